```python
import math
import jax, jax.numpy as jnp
from jax import lax
import numpy as np

D_MODEL = 4096
BATCH = 4
SEQ = 4096
DEPTH = 2
DEC_BATCH = 32
DEC_SEQ = 32
PAST_LEN = 2048

CHUNK = 64
N_MIXERS = 2
N_A_LAYERS = (DEPTH + 1) // 2
N_B_LAYERS = DEPTH // 2
A_BLOCK = 128
A_WIDTH = D_MODEL
A_GROUPS = 32
A_GROUP_DIM = A_WIDTH // A_GROUPS
B_EXPAND = 128
B_HEADS = D_MODEL // B_EXPAND
B_DK = B_EXPAND
B_DV = D_MODEL // B_HEADS
B_QK = B_HEADS * B_DK
B_V = B_HEADS * B_DV
B_CHUNK = CHUNK
D_FF = 11008
CONV_W = 3
ALPHA = (2 * DEPTH) ** 0.25
BETA = (8 * DEPTH) ** -0.25
LN_EPS = 1e-5
RMS_EPS = 1e-6

kernel_name = "hybrid_gmlp_hgrn2_convffn_stream_step"


def layer_norm(x, g, b):
    xf = x.astype(jnp.float32)
    mu = jnp.mean(xf, axis=-1, keepdims=True)
    var = jnp.mean(jnp.square(xf - mu), axis=-1, keepdims=True)
    y = (xf - mu) * lax.rsqrt(var + LN_EPS) * g.astype(jnp.float32) + b.astype(jnp.float32)
    return y.astype(x.dtype)


def chunk_mlp_mixer(x, w_in, ln_g, ln_b, w_s, b_s, w_out, blk):
    bn, L, _ = x.shape
    z = jax.nn.gelu(x @ w_in)
    u, v = jnp.split(z, 2, axis=-1)
    v = layer_norm(v, ln_g, ln_b)
    nblk = L // blk
    vb = v.reshape(bn, nblk, blk, A_GROUPS, A_GROUP_DIM)
    causal = jnp.tril(jnp.ones((blk, blk), dtype=bool))
    ws = jnp.where(causal, w_s[:, :blk, :blk], 0.0).astype(v.dtype)
    mixed = jnp.einsum('gts,bnsgc->bntgc', ws, vb) + b_s[:, :blk].T[None, None, :, :, None].astype(v.dtype)
    gated = u * mixed.reshape(bn, L, A_WIDTH)
    return gated @ w_out, v


def gla_chunkwise(q, k, v, logf, S0, chunk):
    bn, L, H, _ = q.shape
    n = L // chunk

    def to_chunks(a):
        return a.reshape(bn, n, chunk, H, a.shape[-1]).transpose(1, 0, 3, 2, 4)

    qc, kc, vc, gc = to_chunks(q), to_chunks(k), to_chunks(v), to_chunks(logf)
    causal = jnp.tril(jnp.ones((chunk, chunk), dtype=bool))[:, :, None]

    def step(S, inp):
        qb, kb, vb, gb = inp
        b = jnp.cumsum(gb, axis=2)
        diff = b[:, :, :, None, :] - b[:, :, None, :, :]
        decay = jnp.exp(jnp.where(causal, diff, -jnp.inf))
        att = jnp.einsum('bhtd,bhtsd,bhsd->bhts', qb, decay, kb)
        o_intra = jnp.einsum('bhts,bhse->bhte', att, vb)
        o_inter = jnp.einsum('bhtd,bhde->bhte', qb * jnp.exp(b), S)
        b_last = b[:, :, -1:, :]
        k_dec = kb * jnp.exp(b_last - b)
        S_new = jnp.exp(b_last[:, :, 0, :])[..., None] * S + jnp.einsum('bhsd,bhse->bhde', k_dec, vb)
        return S_new, o_intra + o_inter

    S_fin, oc = lax.scan(step, S0, (qc, kc, vc, gc))
    o = oc.transpose(1, 0, 3, 2, 4).reshape(bn, L, H, -1)
    return o, S_fin


def hgrn2_mixer(x, S0, lb, w_in, gn_g, w_out, chunk):
    f32 = jnp.float32
    bn, L, _ = x.shape
    h = x @ w_in
    q, zf, i_in, g = jnp.split(h, [B_QK, 2 * B_QK, 2 * B_QK + B_V], axis=-1)
    lbf = lb.astype(f32)
    f = lbf + (1.0 - lbf) * jax.nn.sigmoid(zf.astype(f32))
    logf = jnp.log(f).reshape(bn, L, B_HEADS, B_DK)
    k = (1.0 - f).reshape(bn, L, B_HEADS, B_DK)
    qh = q.astype(f32).reshape(bn, L, B_HEADS, B_DK)
    vh = i_in.astype(f32).reshape(bn, L, B_HEADS, B_DV)
    o, S = gla_chunkwise(qh, k, vh, logf, S0.astype(f32), chunk)
    o = o * lax.rsqrt(jnp.mean(o * o, axis=-1, keepdims=True) + RMS_EPS)
    o = o.reshape(bn, L, B_V) * gn_g.astype(f32) * jax.nn.silu(g.astype(f32))
    return o.astype(x.dtype) @ w_out, S.astype(S0.dtype)


def conv_ffn(x, conv_state, w_up, conv_w, conv_b, w_down):
    L = x.shape[1]
    a, u = jnp.split(x @ w_up, 2, axis=-1)
    ext = jnp.concatenate([conv_state.astype(a.dtype), a], axis=1)
    c = conv_b + ext[:, 0:L] * conv_w[0]
    for j in range(1, CONV_W):
        c = c + ext[:, j:j + L] * conv_w[j]
    y = (jax.nn.silu(c) * u) @ w_down
    return y, ext[:, -(CONV_W - 1):]


def setup_inputs(seed: int = 0) -> dict:
    key = jax.random.key(seed)
    ks = jax.random.split(key, 24)
    f32 = jnp.float32
    D = D_MODEL

    def nrm(k, shape, scale):
        return jax.random.normal(k, shape, f32) * scale

    return {
        "x_prompt": nrm(ks[0], (BATCH, SEQ, D), 1.0),
        "x_sample": nrm(ks[1], (DEC_BATCH, DEC_SEQ, D), 1.0),
        "state_hgrn": nrm(ks[2], (N_B_LAYERS, DEC_BATCH, B_HEADS, B_DK, B_DV), 0.5),
        "cache_conv": nrm(ks[3], (DEPTH, DEC_BATCH, CONV_W - 1, D_FF), 1.0),
        "a_w_in": nrm(ks[4], (N_A_LAYERS, D, 2 * A_WIDTH), D ** -0.5),
        "a_ln_g": 1.0 + nrm(ks[5], (N_A_LAYERS, A_WIDTH), 0.1),
        "a_ln_b": nrm(ks[6], (N_A_LAYERS, A_WIDTH), 0.02),
        "a_w_s": nrm(ks[7], (N_A_LAYERS, A_GROUPS, A_BLOCK, A_BLOCK), A_BLOCK ** -0.5),
        "a_b_s": 1.0 + nrm(ks[8], (N_A_LAYERS, A_GROUPS, A_BLOCK), 0.1),
        "a_w_out": nrm(ks[9], (N_A_LAYERS, A_WIDTH, D), A_WIDTH ** -0.5 * BETA),
        "b_w_in": nrm(ks[10], (N_B_LAYERS, D, 2 * B_QK + B_V + D), D ** -0.5),
        "b_lb": 1.0 + nrm(ks[11], (DEPTH, B_QK), 0.5),
        "b_gn_g": 1.0 + nrm(ks[12], (N_B_LAYERS, B_V), 0.1),
        "b_w_out": nrm(ks[13], (N_B_LAYERS, B_V, D), B_V ** -0.5 * BETA),
        "f_w_up": nrm(ks[14], (DEPTH, D, 2 * D_FF), D ** -0.5),
        "f_conv_w": nrm(ks[15], (DEPTH, CONV_W, D_FF), CONV_W ** -0.5),
        "f_conv_b": nrm(ks[16], (DEPTH, D_FF), 0.01),
        "f_w_down": nrm(ks[17], (DEPTH, D_FF, D), D_FF ** -0.5 * BETA),
        "ln1_g": 1.0 + nrm(ks[18], (DEPTH, D), 0.1),
        "ln1_b": nrm(ks[19], (DEPTH, D), 0.02),
        "ln2_g": 1.0 + nrm(ks[20], (DEPTH, D), 0.1),
        "ln2_b": nrm(ks[21], (DEPTH, D), 0.02),
    }


def reference(x_prompt, x_sample, state_hgrn, cache_conv,
              a_w_in, a_ln_g, a_ln_b, a_w_s, a_b_s, a_w_out,
              b_w_in, b_lb, b_gn_g, b_w_out,
              f_w_up, f_conv_w, f_conv_b, f_w_down,
              ln1_g, ln1_b, ln2_g, ln2_b):
    f32 = jnp.float32
    p = jax.nn.softmax(b_lb.astype(f32), axis=0)
    lower_bounds = jnp.cumsum(p, axis=0) - p[0]

    xp, xs = x_prompt, x_sample
    v_rows, s_prompt, s_sample, c_prompt, c_sample = [], [], [], [], []
    for i in range(DEPTH):
        j = i // N_MIXERS
        if i % N_MIXERS == 0:
            mp, _ = chunk_mlp_mixer(xp, a_w_in[j], a_ln_g[j], a_ln_b[j], a_w_s[j], a_b_s[j], a_w_out[j], A_BLOCK)
            ms, vs = chunk_mlp_mixer(xs, a_w_in[j], a_ln_g[j], a_ln_b[j], a_w_s[j], a_b_s[j], a_w_out[j], xs.shape[1])
            v_rows.append(vs)
        else:
            S0 = jnp.zeros((xp.shape[0], B_HEADS, B_DK, B_DV), state_hgrn.dtype)
            mp, Sp = hgrn2_mixer(xp, S0, lower_bounds[i], b_w_in[j], b_gn_g[j], b_w_out[j], B_CHUNK)
            ms, Ss = hgrn2_mixer(xs, state_hgrn[j], lower_bounds[i], b_w_in[j], b_gn_g[j], b_w_out[j], xs.shape[1])
            s_prompt.append(Sp)
            s_sample.append(Ss)
        xp = layer_norm(ALPHA * xp + mp, ln1_g[i], ln1_b[i])
        xs = layer_norm(ALPHA * xs + ms, ln1_g[i], ln1_b[i])

        c0 = jnp.zeros((xp.shape[0], CONV_W - 1, D_FF), cache_conv.dtype)
        fp, cp = conv_ffn(xp, c0, f_w_up[i], f_conv_w[i], f_conv_b[i], f_w_down[i])
        fs, cs = conv_ffn(xs, cache_conv[i], f_w_up[i], f_conv_w[i], f_conv_b[i], f_w_down[i])
        c_prompt.append(cp)
        c_sample.append(cs)
        xp = layer_norm(ALPHA * xp + fp, ln2_g[i], ln2_b[i])
        xs = layer_norm(ALPHA * xs + fs, ln2_g[i], ln2_b[i])

    return (xp, xs, jnp.stack(v_rows), jnp.stack(s_prompt), jnp.stack(s_sample), jnp.stack(c_prompt), jnp.stack(c_sample))
```

```python
import functools

import jax
import jax.numpy as jnp
from jax import lax
from jax.experimental import pallas as pl
from jax.experimental.pallas import tpu as pltpu

F32 = jnp.float32
BF16 = jnp.bfloat16

LN_EPS = 1e-5
RMS_EPS = 1e-6
LANES = 128
SUBLANES = 8
VMEM_LIMIT_BYTES = 56 * 1024 * 1024
NEG_BIG = -1e30


def _cparams(*sem):
    return pltpu.CompilerParams(dimension_semantics=sem, vmem_limit_bytes=VMEM_LIMIT_BYTES)


def _tile(n, pref):
    t = min(n, pref)
    while n % t:
        t -= LANES
    return t


def _dot(a, b):
    return jnp.dot(a, b, preferred_element_type=F32)


def _log2(n):
    assert n > 0 and n & (n - 1) == 0, f"{n} is not a power of two"
    return n.bit_length() - 1


def _block_id(idx, size):
    return lax.shift_right_logical(idx, _log2(size))


def _layer_norm_rows(x, g, b):
    mu = jnp.mean(x, axis=-1, keepdims=True)
    xc = x - mu
    var = jnp.mean(xc * xc, axis=-1, keepdims=True)
    return xc * lax.rsqrt(var + LN_EPS) * g + b


def _a_in_kernel(x_ref, wu_ref, wv_ref, u_ref, v_ref):
    x = x_ref[...]
    u_ref[...] = jax.nn.gelu(_dot(x, wu_ref[...])).astype(u_ref.dtype)
    v_ref[...] = jax.nn.gelu(_dot(x, wv_ref[...]))


def _a_in(xb, w_in):
    T, D = xb.shape
    W = w_in.shape[1] // 2
    tm, tn = _tile(T, 1024), _tile(W, 512)
    nn = W // tn
    return pl.pallas_call(
        _a_in_kernel,
        grid=(T // tm, nn),
        in_specs=[pl.BlockSpec((tm, D), lambda m, n: (m, 0)),
                  pl.BlockSpec((D, tn), lambda m, n: (0, n)),
                  pl.BlockSpec((D, tn), lambda m, n: (0, n + nn))],
        out_specs=[pl.BlockSpec((tm, tn), lambda m, n: (m, n)),
                   pl.BlockSpec((tm, tn), lambda m, n: (m, n))],
        out_shape=[jax.ShapeDtypeStruct((T, W), BF16), jax.ShapeDtypeStruct((T, W), F32)],
        compiler_params=_cparams("parallel", "arbitrary"),
        name="a_in",
    )(xb, w_in, w_in)


def _spatial_kernel(u_ref, v_ref, g_ref, b_ref, ws_ref, bs_ref, o_ref, vn_ref, *, nb, groups, period):
    vn_ref[...] = _layer_norm_rows(v_ref[...], g_ref[...], b_ref[...])
    row = lax.broadcasted_iota(jnp.int32, (LANES, LANES), 0)
    col = lax.broadcasted_iota(jnp.int32, (LANES, LANES), 1)
    keep = (col <= row) & (_block_id(row, period) == _block_id(col, period))
    for g in range(groups):
        cs = slice(g * LANES, (g + 1) * LANES)
        wsg = jnp.where(keep, ws_ref[g], 0.0).astype(BF16)
        bias = bs_ref[:, g:g + 1]
        for j in range(nb):
            rs = slice(j * LANES, (j + 1) * LANES)
            mixed = _dot(wsg, vn_ref[rs, cs].astype(BF16)) + bias
            o_ref[rs, cs] = (u_ref[rs, cs].astype(F32) * mixed).astype(o_ref.dtype)


def _spatial(u, v, ln_g, ln_b, ws, bs_t, period, want_vn):
    T, W = v.shape
    G = ws.shape[0]
    tb = _tile(T, 2 * LANES)
    row_spec = pl.BlockSpec((tb, W), lambda i: (i, 0))
    vec_spec = pl.BlockSpec((1, W), lambda i: (0, 0))
    kern = functools.partial(_spatial_kernel, nb=tb // LANES, groups=G, period=period)
    common = dict(
        grid=(T // tb,),
        in_specs=[row_spec, row_spec, vec_spec, vec_spec,
                  pl.BlockSpec((G, LANES, LANES), lambda i: (0, 0, 0)),
                  pl.BlockSpec((LANES, G), lambda i: (0, 0))],
        compiler_params=_cparams("parallel"),
    )
    if want_vn:
        return pl.pallas_call(
            kern, out_specs=[row_spec, row_spec],
            out_shape=[jax.ShapeDtypeStruct((T, W), BF16), jax.ShapeDtypeStruct((T, W), F32)],
            name="spatial_vn", **common)(u, v, ln_g, ln_b, ws, bs_t)
    gated = pl.pallas_call(
        kern, out_specs=row_spec, out_shape=jax.ShapeDtypeStruct((T, W), BF16),
        scratch_shapes=[pltpu.VMEM((tb, W), F32)],
        name="spatial", **common)(u, v, ln_g, ln_b, ws, bs_t)
    return gated, None


LN_ROWS = 32


def _ln_inplace(src_ref, g_ref, b_ref, of_ref, ob_ref, rows):
    g = g_ref[...]
    b = b_ref[...]

    def body(i, c):
        rs = pl.ds(pl.multiple_of(i * LN_ROWS, LN_ROWS), LN_ROWS)
        y = _layer_norm_rows(src_ref[rs, :], g, b)
        of_ref[rs, :] = y
        ob_ref[rs, :] = y.astype(ob_ref.dtype)
        return c

    lax.fori_loop(0, rows // LN_ROWS, body, 0)


def _mm_res_ln_kernel(x_ref, w_ref, res_ref, g_ref, b_ref, of_ref, ob_ref, *scratch,
                      alpha, nn, tn, heads):
    n = pl.program_id(1)
    if heads:
        xs_ref, = scratch

        @pl.when(n == 0)
        def _():
            for h in range(heads):
                xs_ref[:, h * LANES:(h + 1) * LANES] = x_ref[h]

        x = xs_ref[...]
    else:
        x = x_ref[...]
    pre = alpha * res_ref[...] + _dot(x, w_ref[...])
    for j in range(nn):
        @pl.when(n == j)
        def _(j=j):
            of_ref[:, j * tn:(j + 1) * tn] = pre

    @pl.when(n == nn - 1)
    def _():
        _ln_inplace(of_ref, g_ref, b_ref, of_ref, ob_ref, of_ref.shape[0])


def _mm_res_ln(x, w, res, ln_g, ln_b, alpha, head_major):
    T, N = res.shape
    K = w.shape[0]
    tm, tn = _tile(T, 512), _tile(N, 512)
    nn = N // tn
    heads = K // LANES if head_major else 0
    if head_major:
        x_spec = pl.BlockSpec((heads, tm, LANES), lambda m, n: (0, m, 0))
        scratch = [pltpu.VMEM((tm, K), BF16)]
    else:
        x_spec = pl.BlockSpec((tm, K), lambda m, n: (m, 0))
        scratch = []
    row_spec = pl.BlockSpec((tm, N), lambda m, n: (m, 0))
    vec_spec = pl.BlockSpec((1, N), lambda m, n: (0, 0))
    return pl.pallas_call(
        functools.partial(_mm_res_ln_kernel, alpha=alpha, nn=nn, tn=tn, heads=heads),
        grid=(T // tm, nn),
        in_specs=[x_spec,
                  pl.BlockSpec((K, tn), lambda m, n: (0, n)),
                  pl.BlockSpec((tm, tn), lambda m, n: (m, n)),
                  vec_spec, vec_spec],
        out_specs=[row_spec, row_spec],
        out_shape=[jax.ShapeDtypeStruct((T, N), F32), jax.ShapeDtypeStruct((T, N), BF16)],
        scratch_shapes=scratch,
        compiler_params=_cparams("parallel", "arbitrary"),
        name="mm_res_ln",
    )(x, w, res, ln_g, ln_b)


def _res_ln_kernel(y_ref, res_ref, g_ref, b_ref, of_ref, ob_ref, *, alpha):
    of_ref[...] = alpha * res_ref[...] + y_ref[...]
    _ln_inplace(of_ref, g_ref, b_ref, of_ref, ob_ref, of_ref.shape[0])


def _res_ln(y, res, ln_g, ln_b, alpha):
    T, N = res.shape
    tm = _tile(T, 256)
    row_spec = pl.BlockSpec((tm, N), lambda m: (m, 0))
    vec_spec = pl.BlockSpec((1, N), lambda m: (0, 0))
    return pl.pallas_call(
        functools.partial(_res_ln_kernel, alpha=alpha),
        grid=(T // tm,),
        in_specs=[row_spec, row_spec, vec_spec, vec_spec],
        out_specs=[row_spec, row_spec],
        out_shape=[jax.ShapeDtypeStruct((T, N), F32), jax.ShapeDtypeStruct((T, N), BF16)],
        compiler_params=_cparams("parallel"),
        name="res_ln",
    )(y, res, ln_g, ln_b)


FFN_DOWN_COLS = 1024


def _ffn_kernel(x_ref, wa_ref, wu_ref, cw_ref, cb_ref, wd_ref, *rest, nseq, slen, tiles_per_seq, carry):
    if carry:
        y_ref, cs_ref, halo_ref = rest
    else:
        st_ref, y_ref, cs_ref = rest
    m = pl.program_id(0)
    f = pl.program_id(1)
    x = x_ref[...]
    tm = x.shape[0]
    a = _dot(x, wa_ref[...])
    u = _dot(x, wu_ref[...])
    tf = a.shape[1]

    if carry:
        @pl.when(m % tiles_per_seq == 0)
        def _():
            halo_ref[f] = jnp.zeros((SUBLANES, tf), F32)

        st = halo_ref[f][None]
    else:
        st = st_ref[...]

    def expand(rows):
        return jnp.broadcast_to(rows, (nseq, slen, tf)).reshape(tm, tf)

    s_m1 = expand(st[:, SUBLANES - 1:SUBLANES, :])
    s_m2 = expand(st[:, SUBLANES - 2:SUBLANES - 1, :])
    _log2(slen)
    pos = lax.broadcasted_iota(jnp.int32, (tm, 1), 0) & (slen - 1)
    prev1 = jnp.where(pos == 0, s_m1, pltpu.roll(a, 1, 0))
    prev2 = jnp.where(pos == 0, s_m2, jnp.where(pos == 1, s_m1, pltpu.roll(a, 2, 0)))
    cw = cw_ref[...]
    c = cb_ref[...] + prev2 * cw[0:1, :] + prev1 * cw[1:2, :] + a * cw[2:3, :]

    tail = a.reshape(nseq, slen, tf)[:, slen - SUBLANES:, :]
    cs_ref[...] = tail
    if carry:
        halo_ref[f] = tail[0]

    gated = (jax.nn.silu(c) * u).astype(BF16)
    d_out = y_ref.shape[1]
    tn = _tile(d_out, FFN_DOWN_COLS)
    for j in range(d_out // tn):
        cs = slice(j * tn, (j + 1) * tn)
        part = _dot(gated, wd_ref[:, cs])

        @pl.when(f == 0)
        def _(part=part, cs=cs):
            y_ref[:, cs] = part

        @pl.when(f > 0)
        def _(part=part, cs=cs):
            y_ref[:, cs] += part


def _ffn(xb, wa, wu, cw, cb, wd, nseq, slen, state):
    T, D = xb.shape
    F = wa.shape[1]
    tf = _tile(F, 512)
    nf = F // tf
    carry = state is None
    if carry:
        tm = _tile(slen, 512)
        tiles_per_seq = slen // tm
        seq_per_tile = 1
        cs_spec = pl.BlockSpec((1, SUBLANES, tf), lambda m, f: (m // tiles_per_seq, 0, f))
        extra_in, extra_specs = [], []
        scratch = [pltpu.VMEM((nf, SUBLANES, tf), F32)]
    else:
        tm = _tile(T, 512)
        assert tm % slen == 0
        tiles_per_seq = 1
        seq_per_tile = tm // slen
        cs_spec = pl.BlockSpec((seq_per_tile, SUBLANES, tf), lambda m, f: (m, 0, f))
        extra_in, extra_specs = [state], [cs_spec]
        scratch = []
    kern = functools.partial(_ffn_kernel, nseq=seq_per_tile, slen=tm // seq_per_tile,
                             tiles_per_seq=tiles_per_seq, carry=carry)
    return pl.pallas_call(
        kern,
        grid=(T // tm, nf),
        in_specs=[pl.BlockSpec((tm, D), lambda m, f: (m, 0)),
                  pl.BlockSpec((D, tf), lambda m, f: (0, f)),
                  pl.BlockSpec((D, tf), lambda m, f: (0, f)),
                  pl.BlockSpec((3, tf), lambda m, f: (0, f)),
                  pl.BlockSpec((1, tf), lambda m, f: (0, f)),
                  pl.BlockSpec((tf, D), lambda m, f: (f, 0))] + extra_specs,
        out_specs=[pl.BlockSpec((tm, D), lambda m, f: (m, 0)), cs_spec],
        out_shape=[jax.ShapeDtypeStruct((T, D), F32), jax.ShapeDtypeStruct((nseq, SUBLANES, F), F32)],
        scratch_shapes=scratch,
        compiler_params=_cparams("arbitrary", "arbitrary"),
        name="ffn_carry" if carry else "ffn_state",
    )(xb, wa, wu, cw, cb, wd, *extra_in)


def _hgrn_in_kernel(x_ref, wq_ref, wf_ref, wi_ref, wg_ref, lb_ref,
                    q_ref, k_ref, lf_ref, v_ref, sg_ref, *, layer, hh):
    x = x_ref[...]
    lbv = lb_ref[...]
    e = jnp.exp(lbv - jnp.max(lbv, axis=0, keepdims=True))
    p = e / jnp.sum(e, axis=0, keepdims=True)
    lb = jnp.sum(p[1:layer + 1, :], axis=0, keepdims=True)

    def put(ref, val):
        for j in range(hh):
            ref[j] = val[:, j * LANES:(j + 1) * LANES].astype(ref.dtype)

    put(q_ref, _dot(x, wq_ref[...]))
    f = lb + (1.0 - lb) * jax.nn.sigmoid(_dot(x, wf_ref[...]))
    put(lf_ref, jnp.log(f))
    put(k_ref, 1.0 - f)
    put(v_ref, _dot(x, wi_ref[...]))
    put(sg_ref, jax.nn.silu(_dot(x, wg_ref[...])))


def _hgrn_in(xb, w_in, b_lb, layer):
    T, D = xb.shape
    QK = w_in.shape[1] // 4
    H = QK // LANES
    tm, tn = _tile(T, 1024), _tile(QK, 256)
    nn = QK // tn
    hh = tn // LANES
    depth = b_lb.shape[0]
    w_specs = [pl.BlockSpec((D, tn), lambda m, n, k=k: (0, n + k * nn)) for k in range(4)]
    o_spec = pl.BlockSpec((hh, tm, LANES), lambda m, n: (n, m, 0))
    hm = lambda dt: jax.ShapeDtypeStruct((H, T, LANES), dt)
    return pl.pallas_call(
        functools.partial(_hgrn_in_kernel, layer=layer, hh=hh),
        grid=(T // tm, nn),
        in_specs=[pl.BlockSpec((tm, D), lambda m, n: (m, 0))] + w_specs
                 + [pl.BlockSpec((depth, tn), lambda m, n: (0, n))],
        out_specs=[o_spec] * 5,
        out_shape=[hm(BF16), hm(BF16), hm(F32), hm(BF16), hm(BF16)],
        compiler_params=_cparams("parallel", "arbitrary"),
        name="hgrn_in",
    )(xb, w_in, w_in, w_in, w_in, b_lb)


def _nt_dot(a, b):
    return lax.dot_general(a, b, (((1,), (1,)), ((), ())), preferred_element_type=F32)


def _tn_dot(a, b):
    return lax.dot_general(a, b, (((0,), (0,)), ((), ())), preferred_element_type=F32)


def _gla_kernel(q_ref, k_ref, lf_ref, v_ref, sg_ref, gn_ref, *rest, C, hh, has_init):
    if has_init:
        s0_ref, o_ref, sout_ref, st_ref, acc_ref = rest
    else:
        o_ref, sout_ref, st_ref, acc_ref = rest
    t = pl.program_id(2)

    @pl.when(t == 0)
    def _():
        if has_init:
            def init(h, c):
                st_ref[h] = s0_ref[0, h].T
                return c
            lax.fori_loop(0, hh, init, 0)
        else:
            st_ref[...] = jnp.zeros(st_ref.shape, F32)

    row = lax.broadcasted_iota(jnp.int32, (C, C), 0)
    col = lax.broadcasted_iota(jnp.int32, (C, C), 1)
    tri = jnp.where(col <= row, 1.0, 0.0).astype(BF16)
    n8 = C // SUBLANES
    sub = lax.broadcasted_iota(jnp.int32, (1, SUBLANES, 1), 1)
    half = C // 2
    hrow = lax.broadcasted_iota(jnp.int32, (half, half), 0)
    hcol = lax.broadcasted_iota(jnp.int32, (half, half), 1)
    widths = []
    w = SUBLANES
    while w < C:
        widths.append(w)
        w *= 2

    def halves(x, w, which):
        return jnp.concatenate([x[(2 * p + which) * w:(2 * p + which + 1) * w] for p in range(C // (2 * w))], axis=0)

    def head(h, c):
        q = q_ref[h].astype(F32)
        k = k_ref[h].astype(F32)
        vb = v_ref[h]
        v32 = vb.astype(F32)
        g = lf_ref[h]
        g1 = g.astype(BF16)
        r1 = g - g1.astype(F32)
        g2 = r1.astype(BF16)
        g3 = (r1 - g2.astype(F32)).astype(BF16)
        b = _dot(tri, g1) + _dot(tri, g2) + _dot(tri, g3)
        b_last = b[C - 1:C, :]

        st = st_ref[h]
        acc_ref[...] = _nt_dot((q * jnp.exp(b)).astype(BF16), st.astype(BF16))
        kd = (k * jnp.exp(b_last - b)).astype(BF16)
        st_new = st * jnp.exp(b_last) + _tn_dot(vb, kd)
        st_ref[h] = st_new
        sout_ref[0, h] = st_new.T

        for w in widths:
            bs, bt = halves(b, w, 0), halves(b, w, 1)
            bref = jnp.concatenate(
                [jnp.broadcast_to(b[(2 * p + 1) * w - 1:(2 * p + 1) * w, :], (w, LANES)) for p in range(C // (2 * w))],
                axis=0)
            qt = (halves(q, w, 1) * jnp.exp(bt - bref)).astype(BF16)
            ks = (halves(k, w, 0) * jnp.exp(bref - bs)).astype(BF16)
            att = _nt_dot(qt, ks)
            att = jnp.where(_block_id(hrow, w) == _block_id(hcol, w), att, 0.0).astype(BF16)
            ot = _dot(att, halves(v32, w, 0).astype(BF16))
            for p in range(C // (2 * w)):
                acc_ref[(2 * p + 1) * w:(2 * p + 2) * w, :] += ot[p * w:(p + 1) * w, :]

        b3 = b.reshape(n8, SUBLANES, LANES)
        q3 = q.reshape(n8, SUBLANES, LANES)
        k3 = k.reshape(n8, SUBLANES, LANES)
        v3 = v32.reshape(n8, SUBLANES, LANES)
        o3 = jnp.zeros((n8, SUBLANES, LANES), F32)
        for s in range(SUBLANES):
            d = jnp.where(sub >= s, b3 - b3[:, s:s + 1, :], NEG_BIG)
            e = q3 * jnp.exp(d) * k3[:, s:s + 1, :]
            o3 = o3 + jnp.sum(e, axis=-1, keepdims=True) * v3[:, s:s + 1, :]

        o = acc_ref[...] + o3.reshape(C, LANES)
        o = o * lax.rsqrt(jnp.mean(o * o, axis=-1, keepdims=True) + RMS_EPS)
        o_ref[h] = (o * gn_ref[h] * sg_ref[h].astype(F32)).astype(o_ref.dtype)
        return c

    lax.fori_loop(0, hh, head, 0)


def _gla(q, k, lf, v, sg, gn, nseq, slen, s0):
    H, T, _ = q.shape
    C = _tile(slen, 256)
    tiles = slen // C
    hh = _tile(H * LANES, 16 * LANES) // LANES
    row_spec = pl.BlockSpec((hh, C, LANES), lambda s, g, t: (g, s * tiles + t, 0))
    st_spec = pl.BlockSpec((1, hh, LANES, LANES), lambda s, g, t: (s, g, 0, 0))
    in_specs = [row_spec] * 5 + [pl.BlockSpec((hh, 1, LANES), lambda s, g, t: (g, 0, 0))]
    args = [q, k, lf, v, sg, gn]
    if s0 is not None:
        in_specs.append(st_spec)
        args.append(s0)
    return pl.pallas_call(
        functools.partial(_gla_kernel, C=C, hh=hh, has_init=s0 is not None),
        grid=(nseq, H // hh, tiles),
        in_specs=in_specs,
        out_specs=[row_spec, st_spec],
        out_shape=[jax.ShapeDtypeStruct((H, T, LANES), BF16),
                   jax.ShapeDtypeStruct((nseq, H, LANES, LANES), F32)],
        scratch_shapes=[pltpu.VMEM((hh, LANES, LANES), F32), pltpu.VMEM((C, LANES), F32)],
        compiler_params=_cparams("parallel", "parallel", "arbitrary"),
        name="gla_init" if s0 is not None else "gla",
    )(*args)


FF_ALIGN = 512


def kernel(x_prompt, x_sample, state_hgrn, cache_conv, a_w_in, a_ln_g, a_ln_b, a_w_s, a_b_s, a_w_out,
           b_w_in, b_lb, b_gn_g, b_w_out, f_w_up, f_conv_w, f_conv_b, f_w_down, ln1_g, ln1_b, ln2_g, ln2_b):
    depth = ln1_g.shape[0]
    alpha = (2 * depth) ** 0.25
    bp, lp, D = x_prompt.shape
    bs, ls, _ = x_sample.shape
    blk = a_w_s.shape[-1]
    assert blk == LANES and lp % blk == 0 and blk % ls == 0 and (bs * ls) % blk == 0
    d_ff = f_conv_w.shape[-1]
    f_pad = -d_ff % FF_ALIGN
    H = b_gn_g.shape[-1] // LANES

    xp = x_prompt.reshape(bp * lp, D)
    xs = x_sample.reshape(bs * ls, D)
    groups = [[xp, xp.astype(BF16), bp, lp], [xs, xs.astype(BF16), bs, ls]]

    v_rows, s_prompt, s_sample, c_prompt, c_sample = [], [], [], [], []
    for i in range(depth):
        j = i // 2
        row = lambda a: a[i][None, :]
        if i % 2 == 0:
            w_in = a_w_in[j].astype(BF16)
            w_out = a_w_out[j].astype(BF16)
            ln_g, ln_b = a_ln_g[j][None, :], a_ln_b[j][None, :]
            reps = blk // ls
            ws_variants = [a_w_s[j], jnp.tile(a_w_s[j][:, :ls, :ls], (1, reps, reps))]
            bs_variants = [a_b_s[j].T, jnp.tile(a_b_s[j][:, :ls], (1, reps)).T]
            for gi, grp in enumerate(groups):
                xf, xb, nseq, slen = grp
                u, v = _a_in(xb, w_in)
                gated, vn = _spatial(u, v, ln_g, ln_b, ws_variants[gi], bs_variants[gi],
                                     period=min(slen, blk), want_vn=gi == 1)
                if gi == 1:
                    v_rows.append(vn.reshape(nseq, slen, -1))
                grp[0], grp[1] = _mm_res_ln(gated, w_out, xf, row(ln1_g), row(ln1_b), alpha, head_major=False)
        else:
            w_in = b_w_in[j].astype(BF16)
            w_out = b_w_out[j].astype(BF16)
            gn = b_gn_g[j].reshape(H, 1, LANES)
            for gi, grp in enumerate(groups):
                xf, xb, nseq, slen = grp
                q, k, lf, v, sg = _hgrn_in(xb, w_in, b_lb, i)
                o, s_fin = _gla(q, k, lf, v, sg, gn, nseq, slen, state_hgrn[j] if gi == 1 else None)
                (s_sample if gi == 1 else s_prompt).append(s_fin)
                grp[0], grp[1] = _mm_res_ln(o, w_out, xf, row(ln1_g), row(ln1_b), alpha, head_major=True)

        pad_cols = lambda a: jnp.pad(a, ((0, 0), (0, f_pad)))
        wa = pad_cols(f_w_up[i][:, :d_ff].astype(BF16))
        wu = pad_cols(f_w_up[i][:, d_ff:].astype(BF16))
        cw = pad_cols(f_conv_w[i])
        cb = pad_cols(f_conv_b[i][None, :])
        wd = jnp.pad(f_w_down[i].astype(BF16), ((0, f_pad), (0, 0)))
        for gi, grp in enumerate(groups):
            xf, xb, nseq, slen = grp
            state = None
            if gi == 1:
                state = jnp.pad(cache_conv[i], ((0, 0), (SUBLANES - 2, 0), (0, f_pad)))
            y, tail = _ffn(xb, wa, wu, cw, cb, wd, nseq, slen, state)
            (c_sample if gi == 1 else c_prompt).append(tail[:, SUBLANES - 2:, :d_ff])
            grp[0], grp[1] = _res_ln(y, xf, row(ln2_g), row(ln2_b), alpha)

    return (groups[0][0].reshape(bp, lp, D), groups[1][0].reshape(bs, ls, D), jnp.stack(v_rows),
            jnp.stack(s_prompt), jnp.stack(s_sample), jnp.stack(c_prompt), jnp.stack(c_sample))
```

```python
import functools

import jax
import jax.numpy as jnp
from jax import lax
from jax.experimental import pallas as pl
from jax.experimental.pallas import tpu as pltpu

F32 = jnp.float32
BF16 = jnp.bfloat16

LN_EPS = 1e-5
RMS_EPS = 1e-6
LANES = 128
SUBLANES = 8
VMEM_LIMIT_BYTES = 56 * 1024 * 1024
NEG_BIG = -1e30


def _cparams(*sem):
    return pltpu.CompilerParams(dimension_semantics=sem, vmem_limit_bytes=VMEM_LIMIT_BYTES)


def _tile(n, pref):
    t = min(n, pref)
    while n % t:
        t -= LANES
    return t


def _dot(a, b):
    return jnp.dot(a, b, preferred_element_type=F32)


def _log2(n):
    assert n > 0 and n & (n - 1) == 0, f"{n} is not a power of two"
    return n.bit_length() - 1


def _block_id(idx, size):
    return lax.shift_right_logical(idx, _log2(size))


def _layer_norm_rows(x, g, b):
    mu = jnp.mean(x, axis=-1, keepdims=True)
    xc = x - mu
    var = jnp.mean(xc * xc, axis=-1, keepdims=True)
    return xc * lax.rsqrt(var + LN_EPS) * g + b


def _a_in_kernel(x_ref, wu_ref, wv_ref, u_ref, v_ref):
    x = x_ref[...]
    u_ref[...] = jax.nn.gelu(_dot(x, wu_ref[...])).astype(u_ref.dtype)
    v_ref[...] = jax.nn.gelu(_dot(x, wv_ref[...]))


def _a_in(xb, w_in):
    T, D = xb.shape
    W = w_in.shape[1] // 2
    tm, tn = _tile(T, 1024), _tile(W, 512)
    nn = W // tn
    return pl.pallas_call(
        _a_in_kernel,
        grid=(T // tm, nn),
        in_specs=[pl.BlockSpec((tm, D), lambda m, n: (m, 0)),
                  pl.BlockSpec((D, tn), lambda m, n: (0, n)),
                  pl.BlockSpec((D, tn), lambda m, n: (0, n + nn))],
        out_specs=[pl.BlockSpec((tm, tn), lambda m, n: (m, n)),
                   pl.BlockSpec((tm, tn), lambda m, n: (m, n))],
        out_shape=[jax.ShapeDtypeStruct((T, W), BF16), jax.ShapeDtypeStruct((T, W), F32)],
        compiler_params=_cparams("parallel", "arbitrary"),
        name="a_in",
    )(xb, w_in, w_in)


def _spatial_kernel(u_ref, v_ref, g_ref, b_ref, ws_ref, bs_ref, o_ref, vn_ref, *, nb, groups, period):
    vn_ref[...] = _layer_norm_rows(v_ref[...], g_ref[...], b_ref[...])
    row = lax.broadcasted_iota(jnp.int32, (LANES, LANES), 0)
    col = lax.broadcasted_iota(jnp.int32, (LANES, LANES), 1)
    keep = (col <= row) & (_block_id(row, period) == _block_id(col, period))
    for g in range(groups):
        cs = slice(g * LANES, (g + 1) * LANES)
        wsg = jnp.where(keep, ws_ref[g], 0.0).astype(BF16)
        bias = bs_ref[:, g:g + 1]
        for j in range(nb):
            rs = slice(j * LANES, (j + 1) * LANES)
            mixed = _dot(wsg, vn_ref[rs, cs].astype(BF16)) + bias
            o_ref[rs, cs] = (u_ref[rs, cs].astype(F32) * mixed).astype(o_ref.dtype)


def _spatial(u, v, ln_g, ln_b, ws, bs_t, period, want_vn):
    T, W = v.shape
    G = ws.shape[0]
    tb = _tile(T, 2 * LANES)
    row_spec = pl.BlockSpec((tb, W), lambda i: (i, 0))
    vec_spec = pl.BlockSpec((1, W), lambda i: (0, 0))
    kern = functools.partial(_spatial_kernel, nb=tb // LANES, groups=G, period=period)
    common = dict(
        grid=(T // tb,),
        in_specs=[row_spec, row_spec, vec_spec, vec_spec,
                  pl.BlockSpec((G, LANES, LANES), lambda i: (0, 0, 0)),
                  pl.BlockSpec((LANES, G), lambda i: (0, 0))],
        compiler_params=_cparams("parallel"),
    )
    if want_vn:
        return pl.pallas_call(
            kern, out_specs=[row_spec, row_spec],
            out_shape=[jax.ShapeDtypeStruct((T, W), BF16), jax.ShapeDtypeStruct((T, W), F32)],
            name="spatial_vn", **common)(u, v, ln_g, ln_b, ws, bs_t)
    gated = pl.pallas_call(
        kern, out_specs=row_spec, out_shape=jax.ShapeDtypeStruct((T, W), BF16),
        scratch_shapes=[pltpu.VMEM((tb, W), F32)],
        name="spatial", **common)(u, v, ln_g, ln_b, ws, bs_t)
    return gated, None


LN_ROWS = 32


def _ln_inplace(src_ref, g_ref, b_ref, of_ref, ob_ref, rows):
    g = g_ref[...]
    b = b_ref[...]

    def body(i, c):
        rs = pl.ds(pl.multiple_of(i * LN_ROWS, LN_ROWS), LN_ROWS)
        y = _layer_norm_rows(src_ref[rs, :], g, b)
        of_ref[rs, :] = y
        ob_ref[rs, :] = y.astype(ob_ref.dtype)
        return c

    lax.fori_loop(0, rows // LN_ROWS, body, 0)


def _mm_res_ln_kernel(x_ref, w_ref, res_ref, g_ref, b_ref, of_ref, ob_ref, *scratch,
                      alpha, nn, tn, heads):
    n = pl.program_id(1)
    if heads:
        xs_ref, = scratch

        @pl.when(n == 0)
        def _():
            for h in range(heads):
                xs_ref[:, h * LANES:(h + 1) * LANES] = x_ref[h]

        x = xs_ref[...]
    else:
        x = x_ref[...]
    pre = alpha * res_ref[...] + _dot(x, w_ref[...])
    of_ref[:, pl.ds(pl.multiple_of(n * tn, tn), tn)] = pre

    @pl.when(n == nn - 1)
    def _():
        _ln_inplace(of_ref, g_ref, b_ref, of_ref, ob_ref, of_ref.shape[0])


def _mm_res_ln(x, w, res, ln_g, ln_b, alpha, head_major):
    T, N = res.shape
    K = w.shape[0]
    tm = _tile(T, 512)
    tn = _tile(N, 1024 if K <= 4096 else 512)
    nn = N // tn
    heads = K // LANES if head_major else 0
    once = dict(pipeline_mode=pl.Buffered(1))
    if head_major:
        x_spec = pl.BlockSpec((heads, tm, LANES), lambda m, n: (0, m, 0), **once)
        scratch = [pltpu.VMEM((tm, K), BF16)]
    else:
        x_spec = pl.BlockSpec((tm, K), lambda m, n: (m, 0), **once)
        scratch = []
    row_spec = pl.BlockSpec((tm, N), lambda m, n: (m, 0), **once)
    vec_spec = pl.BlockSpec((1, N), lambda m, n: (0, 0))
    return pl.pallas_call(
        functools.partial(_mm_res_ln_kernel, alpha=alpha, nn=nn, tn=tn, heads=heads),
        grid=(T // tm, nn),
        in_specs=[x_spec,
                  pl.BlockSpec((K, tn), lambda m, n: (0, n)),
                  pl.BlockSpec((tm, tn), lambda m, n: (m, n)),
                  vec_spec, vec_spec],
        out_specs=[row_spec, row_spec],
        out_shape=[jax.ShapeDtypeStruct((T, N), F32), jax.ShapeDtypeStruct((T, N), BF16)],
        scratch_shapes=scratch,
        compiler_params=_cparams("parallel", "arbitrary"),
        name="mm_res_ln",
    )(x, w, res, ln_g, ln_b)


FFN_SUB_COLS = 256


def _ffn_up_kernel(x_ref, wa_ref, wu_ref, cw_ref, cb_ref, *rest, nseq, slen, tiles_per_seq, carry):
    if carry:
        g_ref, cs_ref, halo_ref = rest
    else:
        st_ref, g_ref, cs_ref = rest
    m = pl.program_id(0)
    f = pl.program_id(1)
    tm, tf = g_ref.shape
    sub = _tile(tf, FFN_SUB_COLS)

    if carry:
        @pl.when(m % tiles_per_seq == 0)
        def _():
            halo_ref[f] = jnp.zeros((SUBLANES, tf), F32)

    _log2(slen)
    pos = lax.broadcasted_iota(jnp.int32, (tm, 1), 0) & (slen - 1)
    x = x_ref[...]
    for j in range(tf // sub):
        cs = slice(j * sub, (j + 1) * sub)
        a = _dot(x, wa_ref[:, cs])
        u = _dot(x, wu_ref[:, cs])
        if carry:
            st = halo_ref[f, :, cs][None]
        else:
            st = st_ref[:, :, cs]

        def expand(rows):
            return jnp.broadcast_to(rows, (nseq, slen, sub)).reshape(tm, sub)

        s_m1 = expand(st[:, SUBLANES - 1:SUBLANES, :])
        s_m2 = expand(st[:, SUBLANES - 2:SUBLANES - 1, :])
        prev1 = jnp.where(pos == 0, s_m1, pltpu.roll(a, 1, 0))
        prev2 = jnp.where(pos == 0, s_m2, jnp.where(pos == 1, s_m1, pltpu.roll(a, 2, 0)))
        cw = cw_ref[:, cs]
        c = cb_ref[:, cs] + prev2 * cw[0:1, :] + prev1 * cw[1:2, :] + a * cw[2:3, :]

        tail = a.reshape(nseq, slen, sub)[:, slen - SUBLANES:, :]
        cs_ref[:, :, cs] = tail
        if carry:
            halo_ref[f, :, cs] = tail[0]
        g_ref[:, cs] = (jax.nn.silu(c) * u).astype(g_ref.dtype)


def _ffn_up(xb, wa, wu, cw, cb, nseq, slen, state):
    T, D = xb.shape
    F = wa.shape[1]
    tf = _tile(F, 512)
    nf = F // tf
    carry = state is None
    if carry:
        tm = _tile(slen, 1024)
        tiles_per_seq = slen // tm
        seq_per_tile = 1
        cs_spec = pl.BlockSpec((1, SUBLANES, tf), lambda m, f: (m, 0, f))
        extra_in, extra_specs = [], []
        scratch = [pltpu.VMEM((nf, SUBLANES, tf), F32)]
    else:
        tm = _tile(T, 1024)
        assert tm % slen == 0
        tiles_per_seq = 1
        seq_per_tile = tm // slen
        cs_spec = pl.BlockSpec((seq_per_tile, SUBLANES, tf), lambda m, f: (m, 0, f))
        extra_in, extra_specs = [state], [cs_spec]
        scratch = []
    kern = functools.partial(_ffn_up_kernel, nseq=seq_per_tile, slen=tm // seq_per_tile,
                             tiles_per_seq=tiles_per_seq, carry=carry)
    n_tails = (T // tm) * seq_per_tile
    gated, tails = pl.pallas_call(
        kern,
        grid=(T // tm, nf),
        in_specs=[pl.BlockSpec((tm, D), lambda m, f: (m, 0)),
                  pl.BlockSpec((D, tf), lambda m, f: (0, f)),
                  pl.BlockSpec((D, tf), lambda m, f: (0, f)),
                  pl.BlockSpec((3, tf), lambda m, f: (0, f)),
                  pl.BlockSpec((1, tf), lambda m, f: (0, f))] + extra_specs,
        out_specs=[pl.BlockSpec((tm, tf), lambda m, f: (m, f)), cs_spec],
        out_shape=[jax.ShapeDtypeStruct((T, F), BF16), jax.ShapeDtypeStruct((n_tails, SUBLANES, F), F32)],
        scratch_shapes=scratch,
        compiler_params=_cparams("arbitrary", "arbitrary"),
        name="ffn_up_carry" if carry else "ffn_up_state",
    )(xb, wa, wu, cw, cb, *extra_in)
    return gated, tails[tiles_per_seq - 1::tiles_per_seq]


def _hgrn_in_kernel(x_ref, wq_ref, wf_ref, wi_ref, wg_ref, lb_ref,
                    q_ref, k_ref, lf_ref, v_ref, sg_ref, *, layer, hh):
    x = x_ref[...]
    lbv = lb_ref[...]
    e = jnp.exp(lbv - jnp.max(lbv, axis=0, keepdims=True))
    p = e / jnp.sum(e, axis=0, keepdims=True)
    lb = jnp.sum(p[1:layer + 1, :], axis=0, keepdims=True)

    def put(ref, val):
        for j in range(hh):
            ref[j] = val[:, j * LANES:(j + 1) * LANES].astype(ref.dtype)

    put(q_ref, _dot(x, wq_ref[...]))
    f = lb + (1.0 - lb) * jax.nn.sigmoid(_dot(x, wf_ref[...]))
    put(lf_ref, jnp.log(f))
    put(k_ref, 1.0 - f)
    put(v_ref, _dot(x, wi_ref[...]))
    put(sg_ref, jax.nn.silu(_dot(x, wg_ref[...])))


def _hgrn_in(xb, w_in, b_lb, layer):
    T, D = xb.shape
    QK = w_in.shape[1] // 4
    H = QK // LANES
    tm, tn = _tile(T, 1024), _tile(QK, 256)
    nn = QK // tn
    hh = tn // LANES
    depth = b_lb.shape[0]
    w_specs = [pl.BlockSpec((D, tn), lambda m, n, k=k: (0, n + k * nn)) for k in range(4)]
    o_spec = pl.BlockSpec((hh, tm, LANES), lambda m, n: (n, m, 0))
    hm = lambda dt: jax.ShapeDtypeStruct((H, T, LANES), dt)
    return pl.pallas_call(
        functools.partial(_hgrn_in_kernel, layer=layer, hh=hh),
        grid=(T // tm, nn),
        in_specs=[pl.BlockSpec((tm, D), lambda m, n: (m, 0))] + w_specs
                 + [pl.BlockSpec((depth, tn), lambda m, n: (0, n))],
        out_specs=[o_spec] * 5,
        out_shape=[hm(BF16), hm(BF16), hm(F32), hm(BF16), hm(BF16)],
        compiler_params=_cparams("parallel", "arbitrary"),
        name="hgrn_in",
    )(xb, w_in, w_in, w_in, w_in, b_lb)


def _nt_dot(a, b):
    return lax.dot_general(a, b, (((1,), (1,)), ((), ())), preferred_element_type=F32)


def _tn_dot(a, b):
    return lax.dot_general(a, b, (((0,), (0,)), ((), ())), preferred_element_type=F32)


def _gla_kernel(q_ref, k_ref, lf_ref, v_ref, sg_ref, gn_ref, *rest, C, hh, has_init):
    if has_init:
        s0_ref, o_ref, sout_ref, st_ref, acc_ref = rest
    else:
        o_ref, sout_ref, st_ref, acc_ref = rest
    t = pl.program_id(2)

    @pl.when(t == 0)
    def _():
        if has_init:
            def init(h, c):
                st_ref[h] = s0_ref[0, h].T
                return c
            lax.fori_loop(0, hh, init, 0)
        else:
            st_ref[...] = jnp.zeros(st_ref.shape, F32)

    row = lax.broadcasted_iota(jnp.int32, (C, C), 0)
    col = lax.broadcasted_iota(jnp.int32, (C, C), 1)
    tri = jnp.where(col <= row, 1.0, 0.0).astype(BF16)
    n8 = C // SUBLANES
    sub = lax.broadcasted_iota(jnp.int32, (1, SUBLANES, 1), 1)
    half = C // 2
    hrow = lax.broadcasted_iota(jnp.int32, (half, half), 0)
    hcol = lax.broadcasted_iota(jnp.int32, (half, half), 1)
    widths = []
    w = SUBLANES
    while w < C:
        widths.append(w)
        w *= 2

    def halves(x, w, which):
        return jnp.concatenate([x[(2 * p + which) * w:(2 * p + which + 1) * w] for p in range(C // (2 * w))], axis=0)

    def head(h, c):
        q = q_ref[h].astype(F32)
        k = k_ref[h].astype(F32)
        vb = v_ref[h]
        v32 = vb.astype(F32)
        g = lf_ref[h]
        g1 = g.astype(BF16)
        r1 = g - g1.astype(F32)
        g2 = r1.astype(BF16)
        g3 = (r1 - g2.astype(F32)).astype(BF16)
        b = _dot(tri, g1) + _dot(tri, g2) + _dot(tri, g3)
        b_last = b[C - 1:C, :]

        st = st_ref[h]
        acc_ref[...] = _nt_dot((q * jnp.exp(b)).astype(BF16), st.astype(BF16))
        kd = (k * jnp.exp(b_last - b)).astype(BF16)
        st_new = st * jnp.exp(b_last) + _tn_dot(vb, kd)
        st_ref[h] = st_new
        sout_ref[0, h] = st_new.T

        for w in widths:
            bs, bt = halves(b, w, 0), halves(b, w, 1)
            bref = jnp.concatenate(
                [jnp.broadcast_to(b[(2 * p + 1) * w - 1:(2 * p + 1) * w, :], (w, LANES)) for p in range(C // (2 * w))],
                axis=0)
            qt = (halves(q, w, 1) * jnp.exp(bt - bref)).astype(BF16)
            ks = (halves(k, w, 0) * jnp.exp(bref - bs)).astype(BF16)
            att = _nt_dot(qt, ks)
            att = jnp.where(_block_id(hrow, w) == _block_id(hcol, w), att, 0.0).astype(BF16)
            ot = _dot(att, halves(v32, w, 0).astype(BF16))
            for p in range(C // (2 * w)):
                acc_ref[(2 * p + 1) * w:(2 * p + 2) * w, :] += ot[p * w:(p + 1) * w, :]

        b3 = b.reshape(n8, SUBLANES, LANES)
        q3 = q.reshape(n8, SUBLANES, LANES)
        k3 = k.reshape(n8, SUBLANES, LANES)
        v3 = v32.reshape(n8, SUBLANES, LANES)
        o3 = jnp.zeros((n8, SUBLANES, LANES), F32)
        for s in range(SUBLANES):
            d = jnp.where(sub >= s, b3 - b3[:, s:s + 1, :], NEG_BIG)
            e = q3 * jnp.exp(d) * k3[:, s:s + 1, :]
            o3 = o3 + jnp.sum(e, axis=-1, keepdims=True) * v3[:, s:s + 1, :]

        o = acc_ref[...] + o3.reshape(C, LANES)
        o = o * lax.rsqrt(jnp.mean(o * o, axis=-1, keepdims=True) + RMS_EPS)
        o_ref[h] = (o * gn_ref[h] * sg_ref[h].astype(F32)).astype(o_ref.dtype)
        return c

    lax.fori_loop(0, hh, head, 0)


def _gla(q, k, lf, v, sg, gn, nseq, slen, s0):
    H, T, _ = q.shape
    C = _tile(slen, 256)
    tiles = slen // C
    hh = _tile(H * LANES, 16 * LANES) // LANES
    row_spec = pl.BlockSpec((hh, C, LANES), lambda s, g, t: (g, s * tiles + t, 0))
    st_spec = pl.BlockSpec((1, hh, LANES, LANES), lambda s, g, t: (s, g, 0, 0))
    in_specs = [row_spec] * 5 + [pl.BlockSpec((hh, 1, LANES), lambda s, g, t: (g, 0, 0))]
    args = [q, k, lf, v, sg, gn]
    if s0 is not None:
        in_specs.append(st_spec)
        args.append(s0)
    return pl.pallas_call(
        functools.partial(_gla_kernel, C=C, hh=hh, has_init=s0 is not None),
        grid=(nseq, H // hh, tiles),
        in_specs=in_specs,
        out_specs=[row_spec, st_spec],
        out_shape=[jax.ShapeDtypeStruct((H, T, LANES), BF16),
                   jax.ShapeDtypeStruct((nseq, H, LANES, LANES), F32)],
        scratch_shapes=[pltpu.VMEM((hh, LANES, LANES), F32), pltpu.VMEM((C, LANES), F32)],
        compiler_params=_cparams("parallel", "parallel", "arbitrary"),
        name="gla_init" if s0 is not None else "gla",
    )(*args)


FF_ALIGN = 512


def kernel(x_prompt, x_sample, state_hgrn, cache_conv, a_w_in, a_ln_g, a_ln_b, a_w_s, a_b_s, a_w_out,
           b_w_in, b_lb, b_gn_g, b_w_out, f_w_up, f_conv_w, f_conv_b, f_w_down, ln1_g, ln1_b, ln2_g, ln2_b):
    depth = ln1_g.shape[0]
    alpha = (2 * depth) ** 0.25
    bp, lp, D = x_prompt.shape
    bs, ls, _ = x_sample.shape
    blk = a_w_s.shape[-1]
    assert blk == LANES and lp % blk == 0 and blk % ls == 0 and (bs * ls) % blk == 0
    d_ff = f_conv_w.shape[-1]
    f_pad = -d_ff % FF_ALIGN
    H = b_gn_g.shape[-1] // LANES

    xp = x_prompt.reshape(bp * lp, D)
    xs = x_sample.reshape(bs * ls, D)
    groups = [[xp, xp.astype(BF16), bp, lp], [xs, xs.astype(BF16), bs, ls]]

    v_rows, s_prompt, s_sample, c_prompt, c_sample = [], [], [], [], []
    for i in range(depth):
        j = i // 2
        row = lambda a: a[i][None, :]
        if i % 2 == 0:
            w_in = a_w_in[j].astype(BF16)
            w_out = a_w_out[j].astype(BF16)
            ln_g, ln_b = a_ln_g[j][None, :], a_ln_b[j][None, :]
            reps = blk // ls
            ws_variants = [a_w_s[j], jnp.tile(a_w_s[j][:, :ls, :ls], (1, reps, reps))]
            bs_variants = [a_b_s[j].T, jnp.tile(a_b_s[j][:, :ls], (1, reps)).T]
            for gi, grp in enumerate(groups):
                xf, xb, nseq, slen = grp
                u, v = _a_in(xb, w_in)
                gated, vn = _spatial(u, v, ln_g, ln_b, ws_variants[gi], bs_variants[gi],
                                     period=min(slen, blk), want_vn=gi == 1)
                if gi == 1:
                    v_rows.append(vn.reshape(nseq, slen, -1))
                grp[0], grp[1] = _mm_res_ln(gated, w_out, xf, row(ln1_g), row(ln1_b), alpha, head_major=False)
        else:
            w_in = b_w_in[j].astype(BF16)
            w_out = b_w_out[j].astype(BF16)
            gn = b_gn_g[j].reshape(H, 1, LANES)
            for gi, grp in enumerate(groups):
                xf, xb, nseq, slen = grp
                q, k, lf, v, sg = _hgrn_in(xb, w_in, b_lb, i)
                o, s_fin = _gla(q, k, lf, v, sg, gn, nseq, slen, state_hgrn[j] if gi == 1 else None)
                (s_sample if gi == 1 else s_prompt).append(s_fin)
                grp[0], grp[1] = _mm_res_ln(o, w_out, xf, row(ln1_g), row(ln1_b), alpha, head_major=True)

        pad_cols = lambda a: jnp.pad(a, ((0, 0), (0, f_pad)))
        wa = pad_cols(f_w_up[i][:, :d_ff].astype(BF16))
        wu = pad_cols(f_w_up[i][:, d_ff:].astype(BF16))
        cw = pad_cols(f_conv_w[i])
        cb = pad_cols(f_conv_b[i][None, :])
        wd = jnp.pad(f_w_down[i].astype(BF16), ((0, f_pad), (0, 0)))
        for gi, grp in enumerate(groups):
            xf, xb, nseq, slen = grp
            state = None
            if gi == 1:
                state = jnp.pad(cache_conv[i], ((0, 0), (SUBLANES - 2, 0), (0, f_pad)))
            gated, tail = _ffn_up(xb, wa, wu, cw, cb, nseq, slen, state)
            (c_sample if gi == 1 else c_prompt).append(tail[:, SUBLANES - 2:, :d_ff])
            grp[0], grp[1] = _mm_res_ln(gated, wd, xf, row(ln2_g), row(ln2_b), alpha, head_major=False)

    return (groups[0][0].reshape(bp, lp, D), groups[1][0].reshape(bs, ls, D), jnp.stack(v_rows),
            jnp.stack(s_prompt), jnp.stack(s_sample), jnp.stack(c_prompt), jnp.stack(c_sample))
```

```python
import functools

import jax
import jax.numpy as jnp
from jax import lax
from jax.experimental import pallas as pl
from jax.experimental.pallas import tpu as pltpu

F32 = jnp.float32
BF16 = jnp.bfloat16

LN_EPS = 1e-5
RMS_EPS = 1e-6
LANES = 128
SUBLANES = 8
VMEM_LIMIT_BYTES = 56 * 1024 * 1024
NEG_BIG = -1e30


def _cparams(*sem):
    return pltpu.CompilerParams(dimension_semantics=sem, vmem_limit_bytes=VMEM_LIMIT_BYTES)


def _tile(n, pref):
    t = min(n, pref)
    while n % t:
        t -= LANES
    return t


def _dot(a, b):
    return jnp.dot(a, b, preferred_element_type=F32)


def _log2(n):
    assert n > 0 and n & (n - 1) == 0, f"{n} is not a power of two"
    return n.bit_length() - 1


def _block_id(idx, size):
    return lax.shift_right_logical(idx, _log2(size))


def _layer_norm_rows(x, g, b):
    mu = jnp.mean(x, axis=-1, keepdims=True)
    xc = x - mu
    var = jnp.mean(xc * xc, axis=-1, keepdims=True)
    return xc * lax.rsqrt(var + LN_EPS) * g + b


def _a_in_kernel(x_ref, wu_ref, wv_ref, u_ref, v_ref):
    x = x_ref[...]
    u_ref[...] = jax.nn.gelu(_dot(x, wu_ref[...])).astype(u_ref.dtype)
    v_ref[...] = jax.nn.gelu(_dot(x, wv_ref[...]))


def _a_in(xb, w_in):
    T, D = xb.shape
    W = w_in.shape[1] // 2
    tm, tn = _tile(T, 1024), _tile(W, 512)
    nn = W // tn
    return pl.pallas_call(
        _a_in_kernel,
        grid=(T // tm, nn),
        in_specs=[pl.BlockSpec((tm, D), lambda m, n: (m, 0)),
                  pl.BlockSpec((D, tn), lambda m, n: (0, n)),
                  pl.BlockSpec((D, tn), lambda m, n: (0, n + nn))],
        out_specs=[pl.BlockSpec((tm, tn), lambda m, n: (m, n)),
                   pl.BlockSpec((tm, tn), lambda m, n: (m, n))],
        out_shape=[jax.ShapeDtypeStruct((T, W), BF16), jax.ShapeDtypeStruct((T, W), F32)],
        compiler_params=_cparams("parallel", "arbitrary"),
        name="a_in",
    )(xb, w_in, w_in)


def _spatial_kernel(u_ref, v_ref, g_ref, b_ref, ws_ref, bs_ref, o_ref, vn_ref, *, nb, groups, period):
    vn_ref[...] = _layer_norm_rows(v_ref[...], g_ref[...], b_ref[...])
    row = lax.broadcasted_iota(jnp.int32, (LANES, LANES), 0)
    col = lax.broadcasted_iota(jnp.int32, (LANES, LANES), 1)
    keep = (col <= row) & (_block_id(row, period) == _block_id(col, period))
    for g in range(groups):
        cs = slice(g * LANES, (g + 1) * LANES)
        wsg = jnp.where(keep, ws_ref[g], 0.0).astype(BF16)
        bias = bs_ref[:, g:g + 1]
        for j in range(nb):
            rs = slice(j * LANES, (j + 1) * LANES)
            mixed = _dot(wsg, vn_ref[rs, cs].astype(BF16)) + bias
            o_ref[rs, cs] = (u_ref[rs, cs].astype(F32) * mixed).astype(o_ref.dtype)


def _spatial(u, v, ln_g, ln_b, ws, bs_t, period, want_vn):
    T, W = v.shape
    G = ws.shape[0]
    tb = _tile(T, 2 * LANES)
    row_spec = pl.BlockSpec((tb, W), lambda i: (i, 0))
    vec_spec = pl.BlockSpec((1, W), lambda i: (0, 0))
    kern = functools.partial(_spatial_kernel, nb=tb // LANES, groups=G, period=period)
    common = dict(
        grid=(T // tb,),
        in_specs=[row_spec, row_spec, vec_spec, vec_spec,
                  pl.BlockSpec((G, LANES, LANES), lambda i: (0, 0, 0)),
                  pl.BlockSpec((LANES, G), lambda i: (0, 0))],
        compiler_params=_cparams("parallel"),
    )
    if want_vn:
        return pl.pallas_call(
            kern, out_specs=[row_spec, row_spec],
            out_shape=[jax.ShapeDtypeStruct((T, W), BF16), jax.ShapeDtypeStruct((T, W), F32)],
            name="spatial_vn", **common)(u, v, ln_g, ln_b, ws, bs_t)
    gated = pl.pallas_call(
        kern, out_specs=row_spec, out_shape=jax.ShapeDtypeStruct((T, W), BF16),
        scratch_shapes=[pltpu.VMEM((tb, W), F32)],
        name="spatial", **common)(u, v, ln_g, ln_b, ws, bs_t)
    return gated, None


LN_ROWS = 32
MM_K_TILE = 2816


def _ln_inplace(src_ref, g_ref, b_ref, of_ref, ob_ref, rows):
    g = g_ref[...]
    b = b_ref[...]

    def body(i, c):
        rs = pl.ds(pl.multiple_of(i * LN_ROWS, LN_ROWS), LN_ROWS)
        y = _layer_norm_rows(src_ref[rs, :], g, b)
        of_ref[rs, :] = y
        ob_ref[rs, :] = y.astype(ob_ref.dtype)
        return c

    lax.fori_loop(0, rows // LN_ROWS, body, 0)


def _mm_res_ln_kernel(x_ref, w_ref, res_ref, g_ref, b_ref, of_ref, ob_ref, *scratch,
                      alpha, nk, nn, tn, heads):
    k = pl.program_id(1)
    n = pl.program_id(2)
    if heads:
        xs_ref, = scratch

        @pl.when(n == 0)
        def _():
            for h in range(heads):
                xs_ref[:, h * LANES:(h + 1) * LANES] = x_ref[h]

        x = xs_ref[...]
    else:
        x = x_ref[...]
    cols = pl.ds(pl.multiple_of(n * tn, tn), tn)

    @pl.when(k == 0)
    def _():
        of_ref[:, cols] = alpha * res_ref[...]

    of_ref[:, cols] += _dot(x, w_ref[...])

    @pl.when((k == nk - 1) & (n == nn - 1))
    def _():
        _ln_inplace(of_ref, g_ref, b_ref, of_ref, ob_ref, of_ref.shape[0])


def _mm_res_ln(x, w, res, ln_g, ln_b, alpha, head_major):
    T, N = res.shape
    K = w.shape[0]
    tm, tn = _tile(T, 1024), _tile(N, 512)
    tk = K if K <= 4096 else _tile(K, MM_K_TILE)
    nk, nn = K // tk, N // tn
    heads = K // LANES if head_major else 0
    once = dict(pipeline_mode=pl.Buffered(1))
    if head_major:
        assert nk == 1
        x_spec = pl.BlockSpec((heads, tm, LANES), lambda m, k, n: (0, m, 0), **once)
        scratch = [pltpu.VMEM((tm, K), BF16)]
    elif nk == 1:
        x_spec = pl.BlockSpec((tm, K), lambda m, k, n: (m, 0), **once)
        scratch = []
    else:
        x_spec = pl.BlockSpec((tm, tk), lambda m, k, n: (m, k))
        scratch = []
    row_spec = pl.BlockSpec((tm, N), lambda m, k, n: (m, 0), **once)
    vec_spec = pl.BlockSpec((1, N), lambda m, k, n: (0, 0))
    res_spec = pl.BlockSpec((tm, tn), lambda m, k, n: (m, jnp.where(k == 0, n, nn - 1)))
    return pl.pallas_call(
        functools.partial(_mm_res_ln_kernel, alpha=alpha, nk=nk, nn=nn, tn=tn, heads=heads),
        grid=(T // tm, nk, nn),
        in_specs=[x_spec,
                  pl.BlockSpec((tk, tn), lambda m, k, n: (k, n)),
                  res_spec, vec_spec, vec_spec],
        out_specs=[row_spec, row_spec],
        out_shape=[jax.ShapeDtypeStruct((T, N), F32), jax.ShapeDtypeStruct((T, N), BF16)],
        scratch_shapes=scratch,
        compiler_params=_cparams("parallel", "arbitrary", "arbitrary"),
        name="mm_res_ln",
    )(x, w, res, ln_g, ln_b)


FFN_SUB_COLS = 256


def _ffn_up_kernel(x_ref, wa_ref, wu_ref, cw_ref, cb_ref, *rest, nseq, slen, tiles_per_seq, carry):
    if carry:
        g_ref, cs_ref, halo_ref = rest
    else:
        st_ref, g_ref, cs_ref = rest
    m = pl.program_id(0)
    f = pl.program_id(1)
    tm, tf = g_ref.shape
    sub = _tile(tf, FFN_SUB_COLS)

    if carry:
        @pl.when(m % tiles_per_seq == 0)
        def _():
            halo_ref[f] = jnp.zeros((SUBLANES, tf), F32)

    _log2(slen)
    pos = lax.broadcasted_iota(jnp.int32, (tm, 1), 0) & (slen - 1)
    x = x_ref[...]
    for j in range(tf // sub):
        cs = slice(j * sub, (j + 1) * sub)
        a = _dot(x, wa_ref[:, cs])
        u = _dot(x, wu_ref[:, cs])
        if carry:
            st = halo_ref[f, :, cs][None]
        else:
            st = st_ref[:, :, cs]

        def expand(rows):
            return jnp.broadcast_to(rows, (nseq, slen, sub)).reshape(tm, sub)

        s_m1 = expand(st[:, SUBLANES - 1:SUBLANES, :])
        s_m2 = expand(st[:, SUBLANES - 2:SUBLANES - 1, :])
        prev1 = jnp.where(pos == 0, s_m1, pltpu.roll(a, 1, 0))
        prev2 = jnp.where(pos == 0, s_m2, jnp.where(pos == 1, s_m1, pltpu.roll(a, 2, 0)))
        cw = cw_ref[:, cs]
        c = cb_ref[:, cs] + prev2 * cw[0:1, :] + prev1 * cw[1:2, :] + a * cw[2:3, :]

        tail = a.reshape(nseq, slen, sub)[:, slen - SUBLANES:, :]
        cs_ref[:, :, cs] = tail
        if carry:
            halo_ref[f, :, cs] = tail[0]
        g_ref[:, cs] = (jax.nn.silu(c) * u).astype(g_ref.dtype)


def _ffn_up(xb, wa, wu, cw, cb, nseq, slen, state):
    T, D = xb.shape
    F = wa.shape[1]
    tf = _tile(F, 512)
    nf = F // tf
    carry = state is None
    if carry:
        tm = _tile(slen, 1024)
        tiles_per_seq = slen // tm
        seq_per_tile = 1
        cs_spec = pl.BlockSpec((1, SUBLANES, tf), lambda m, f: (m, 0, f))
        extra_in, extra_specs = [], []
        scratch = [pltpu.VMEM((nf, SUBLANES, tf), F32)]
    else:
        tm = _tile(T, 1024)
        assert tm % slen == 0
        tiles_per_seq = 1
        seq_per_tile = tm // slen
        cs_spec = pl.BlockSpec((seq_per_tile, SUBLANES, tf), lambda m, f: (m, 0, f))
        extra_in, extra_specs = [state], [cs_spec]
        scratch = []
    kern = functools.partial(_ffn_up_kernel, nseq=seq_per_tile, slen=tm // seq_per_tile,
                             tiles_per_seq=tiles_per_seq, carry=carry)
    n_tails = (T // tm) * seq_per_tile
    gated, tails = pl.pallas_call(
        kern,
        grid=(T // tm, nf),
        in_specs=[pl.BlockSpec((tm, D), lambda m, f: (m, 0)),
                  pl.BlockSpec((D, tf), lambda m, f: (0, f)),
                  pl.BlockSpec((D, tf), lambda m, f: (0, f)),
                  pl.BlockSpec((3, tf), lambda m, f: (0, f)),
                  pl.BlockSpec((1, tf), lambda m, f: (0, f))] + extra_specs,
        out_specs=[pl.BlockSpec((tm, tf), lambda m, f: (m, f)), cs_spec],
        out_shape=[jax.ShapeDtypeStruct((T, F), BF16), jax.ShapeDtypeStruct((n_tails, SUBLANES, F), F32)],
        scratch_shapes=scratch,
        compiler_params=_cparams("arbitrary", "arbitrary"),
        name="ffn_up_carry" if carry else "ffn_up_state",
    )(xb, wa, wu, cw, cb, *extra_in)
    return gated, tails[tiles_per_seq - 1::tiles_per_seq]


def _hgrn_in_kernel(x_ref, wq_ref, wf_ref, wi_ref, wg_ref, lb_ref,
                    q_ref, k_ref, lf_ref, v_ref, sg_ref, *, layer, hh):
    x = x_ref[...]
    lbv = lb_ref[...]
    e = jnp.exp(lbv - jnp.max(lbv, axis=0, keepdims=True))
    p = e / jnp.sum(e, axis=0, keepdims=True)
    lb = jnp.sum(p[1:layer + 1, :], axis=0, keepdims=True)

    def put(ref, val):
        for j in range(hh):
            ref[j] = val[:, j * LANES:(j + 1) * LANES].astype(ref.dtype)

    put(q_ref, _dot(x, wq_ref[...]))
    f = lb + (1.0 - lb) * jax.nn.sigmoid(_dot(x, wf_ref[...]))
    put(lf_ref, jnp.log(f))
    put(k_ref, 1.0 - f)
    put(v_ref, _dot(x, wi_ref[...]))
    put(sg_ref, jax.nn.silu(_dot(x, wg_ref[...])))


def _hgrn_in(xb, w_in, b_lb, layer):
    T, D = xb.shape
    QK = w_in.shape[1] // 4
    H = QK // LANES
    tm, tn = _tile(T, 1024), _tile(QK, 256)
    nn = QK // tn
    hh = tn // LANES
    depth = b_lb.shape[0]
    w_specs = [pl.BlockSpec((D, tn), lambda m, n, k=k: (0, n + k * nn)) for k in range(4)]
    o_spec = pl.BlockSpec((hh, tm, LANES), lambda m, n: (n, m, 0))
    hm = lambda dt: jax.ShapeDtypeStruct((H, T, LANES), dt)
    return pl.pallas_call(
        functools.partial(_hgrn_in_kernel, layer=layer, hh=hh),
        grid=(T // tm, nn),
        in_specs=[pl.BlockSpec((tm, D), lambda m, n: (m, 0))] + w_specs
                 + [pl.BlockSpec((depth, tn), lambda m, n: (0, n))],
        out_specs=[o_spec] * 5,
        out_shape=[hm(BF16), hm(BF16), hm(F32), hm(BF16), hm(BF16)],
        compiler_params=_cparams("parallel", "arbitrary"),
        name="hgrn_in",
    )(xb, w_in, w_in, w_in, w_in, b_lb)


GLA_BLOCK = 32
GLA_MAX_SPREAD = 60.0
GLA_HEADS_PER_ITER = 2

def _nt_dot(a, b):
    return lax.dot_general(a, b, (((1,), (1,)), ((), ())), preferred_element_type=F32)


def _tn_dot(a, b):
    return lax.dot_general(a, b, (((0,), (0,)), ((), ())), preferred_element_type=F32)


def _gla_kernel(q_ref, k_ref, lf_ref, v_ref, sg_ref, gn_ref, *rest, C, hh, has_init):
    if has_init:
        s0_ref, o_ref, sout_ref, st_ref, acc_ref = rest
    else:
        o_ref, sout_ref, st_ref, acc_ref = rest
    t = pl.program_id(2)

    @pl.when(t == 0)
    def _():
        if has_init:
            def init(h, c):
                st_ref[h] = s0_ref[0, h].T
                return c
            lax.fori_loop(0, hh, init, 0)
        else:
            st_ref[...] = jnp.zeros(st_ref.shape, F32)

    row = lax.broadcasted_iota(jnp.int32, (C, C), 0)
    col = lax.broadcasted_iota(jnp.int32, (C, C), 1)
    tri = jnp.where(col <= row, 1.0, 0.0).astype(BF16)
    n8 = C // SUBLANES
    sub = lax.broadcasted_iota(jnp.int32, (1, SUBLANES, 1), 1)
    half = C // 2
    hrow = lax.broadcasted_iota(jnp.int32, (half, half), 0)
    hcol = lax.broadcasted_iota(jnp.int32, (half, half), 1)
    w0 = min(GLA_BLOCK, C)
    nb0 = C // w0
    dr = min(C, LANES)
    drow = lax.broadcasted_iota(jnp.int32, (dr, dr), 0)
    dcol = lax.broadcasted_iota(jnp.int32, (dr, dr), 1)

    def widths(lo, hi):
        out = []
        while lo < hi:
            out.append(lo)
            lo *= 2
        return out

    def halves(x, w, which):
        return jnp.concatenate([x[(2 * p + which) * w:(2 * p + which + 1) * w] for p in range(C // (2 * w))], axis=0)

    def rows_at(x, first, step, reps):
        return jnp.concatenate(
            [jnp.broadcast_to(x[i:i + 1, :], (reps, LANES)) for i in range(first, C, step)], axis=0)

    def load(h):
        q = q_ref[h].astype(F32)
        k = k_ref[h].astype(F32)
        vb = v_ref[h]
        g = lf_ref[h]
        g1 = g.astype(BF16)
        r1 = g - g1.astype(F32)
        g2 = r1.astype(BF16)
        g3 = (r1 - g2.astype(F32)).astype(BF16)
        b = _dot(tri, g1) + _dot(tri, g2) + _dot(tri, g3)
        return q, k, vb, vb.astype(F32), b

    def level(w, q, k, v32, b):
        bref = rows_at(b, w - 1, 2 * w, w)
        qt = (halves(q, w, 1) * jnp.exp(halves(b, w, 1) - bref)).astype(BF16)
        ks = (halves(k, w, 0) * jnp.exp(bref - halves(b, w, 0))).astype(BF16)
        att = _nt_dot(qt, ks)
        if C > 2 * w:
            att = jnp.where(_block_id(hrow, w) == _block_id(hcol, w), att, 0.0)
        return _dot(att.astype(BF16), halves(v32, w, 0).astype(BF16))

    def finish(h, o):
        o = o * lax.rsqrt(jnp.mean(o * o, axis=-1, keepdims=True) + RMS_EPS)
        o_ref[h] = (o * gn_ref[h] * sg_ref[h].astype(F32)).astype(o_ref.dtype)

    def common(h, slot):
        q, k, vb, v32, b = load(h)
        b_last = b[C - 1:C, :]
        st = st_ref[h]
        o_state = _nt_dot((q * jnp.exp(b)).astype(BF16), st.astype(BF16))
        acc_ref[slot] = o_state
        kd = (k * jnp.exp(b_last - b)).astype(BF16)
        st_new = st * jnp.exp(b_last) + _tn_dot(vb, kd)
        st_ref[h] = st_new
        sout_ref[0, h] = st_new.T

        blocks = [o_state[i * w0:(i + 1) * w0] for i in range(nb0)]
        for w in widths(w0, C):
            ot = level(w, q, k, v32, b)
            for p in range(C // (2 * w)):
                for i in range(w // w0):
                    blocks[(2 * p + 1) * w // w0 + i] += ot[p * w + i * w0:p * w + (i + 1) * w0]
        bref = rows_at(b, 0, w0, w0)
        qe = (q * jnp.exp(b - bref)).astype(BF16)
        ke = (k * jnp.exp(jnp.minimum(bref - b, GLA_MAX_SPREAD))).astype(BF16)
        keep = (_block_id(drow, w0) == _block_id(dcol, w0)) & (dcol <= drow)
        for r in range(C // dr):
            rs = slice(r * dr, (r + 1) * dr)
            od = _dot(jnp.where(keep, _nt_dot(qe[rs], ke[rs]), 0.0).astype(BF16), vb[rs])
            for i in range(dr // w0):
                blocks[r * dr // w0 + i] += od[i * w0:(i + 1) * w0]
        finish(h, jnp.concatenate(blocks, axis=0))
        return jnp.max(rows_at(b, 0, w0, 1) - rows_at(b, w0 - 1, w0, 1))

    def exact(h, slot):
        q, k, _, v32, b = load(h)
        for w in widths(SUBLANES, C):
            ot = level(w, q, k, v32, b)
            for p in range(C // (2 * w)):
                acc_ref[slot, (2 * p + 1) * w:(2 * p + 2) * w, :] += ot[p * w:(p + 1) * w, :]
        b3 = b.reshape(n8, SUBLANES, LANES)
        q3 = q.reshape(n8, SUBLANES, LANES)
        k3 = k.reshape(n8, SUBLANES, LANES)
        v3 = v32.reshape(n8, SUBLANES, LANES)
        o3 = jnp.zeros((n8, SUBLANES, LANES), F32)
        for s in range(SUBLANES):
            d = jnp.where(sub >= s, b3 - b3[:, s:s + 1, :], NEG_BIG)
            e = q3 * jnp.exp(d) * k3[:, s:s + 1, :]
            o3 = o3 + jnp.sum(e, axis=-1, keepdims=True) * v3[:, s:s + 1, :]
        finish(h, acc_ref[slot] + o3.reshape(C, LANES))

    def heads(i, c):
        hs = [i * GLA_HEADS_PER_ITER + j for j in range(GLA_HEADS_PER_ITER)]
        spreads = [common(h, slot) for slot, h in enumerate(hs)]
        for slot, (h, spread) in enumerate(zip(hs, spreads)):
            @pl.when(jnp.logical_not(spread <= GLA_MAX_SPREAD))
            def _(h=h, slot=slot):
                exact(h, slot)
        return c

    lax.fori_loop(0, hh // GLA_HEADS_PER_ITER, heads, 0)


def _gla(q, k, lf, v, sg, gn, nseq, slen, s0):
    H, T, _ = q.shape
    C = _tile(slen, 256)
    tiles = slen // C
    hh = _tile(H * LANES, 16 * LANES) // LANES
    row_spec = pl.BlockSpec((hh, C, LANES), lambda s, g, t: (g, s * tiles + t, 0))
    st_spec = pl.BlockSpec((1, hh, LANES, LANES), lambda s, g, t: (s, g, 0, 0))
    in_specs = [row_spec] * 5 + [pl.BlockSpec((hh, 1, LANES), lambda s, g, t: (g, 0, 0))]
    args = [q, k, lf, v, sg, gn]
    if s0 is not None:
        in_specs.append(st_spec)
        args.append(s0)
    return pl.pallas_call(
        functools.partial(_gla_kernel, C=C, hh=hh, has_init=s0 is not None),
        grid=(nseq, H // hh, tiles),
        in_specs=in_specs,
        out_specs=[row_spec, st_spec],
        out_shape=[jax.ShapeDtypeStruct((H, T, LANES), BF16),
                   jax.ShapeDtypeStruct((nseq, H, LANES, LANES), F32)],
        scratch_shapes=[pltpu.VMEM((hh, LANES, LANES), F32), pltpu.VMEM((GLA_HEADS_PER_ITER, C, LANES), F32)],
        compiler_params=_cparams("parallel", "parallel", "arbitrary"),
        name="gla_init" if s0 is not None else "gla",
    )(*args)


FF_ALIGN = 512


def kernel(x_prompt, x_sample, state_hgrn, cache_conv, a_w_in, a_ln_g, a_ln_b, a_w_s, a_b_s, a_w_out,
           b_w_in, b_lb, b_gn_g, b_w_out, f_w_up, f_conv_w, f_conv_b, f_w_down, ln1_g, ln1_b, ln2_g, ln2_b):
    depth = ln1_g.shape[0]
    alpha = (2 * depth) ** 0.25
    bp, lp, D = x_prompt.shape
    bs, ls, _ = x_sample.shape
    blk = a_w_s.shape[-1]
    assert blk == LANES and lp % blk == 0 and blk % ls == 0 and (bs * ls) % blk == 0
    d_ff = f_conv_w.shape[-1]
    f_pad = -d_ff % FF_ALIGN
    H = b_gn_g.shape[-1] // LANES

    xp = x_prompt.reshape(bp * lp, D)
    xs = x_sample.reshape(bs * ls, D)
    groups = [[xp, xp.astype(BF16), bp, lp], [xs, xs.astype(BF16), bs, ls]]

    v_rows, s_prompt, s_sample, c_prompt, c_sample = [], [], [], [], []
    for i in range(depth):
        j = i // 2
        row = lambda a: a[i][None, :]
        if i % 2 == 0:
            w_in = a_w_in[j].astype(BF16)
            w_out = a_w_out[j].astype(BF16)
            ln_g, ln_b = a_ln_g[j][None, :], a_ln_b[j][None, :]
            reps = blk // ls
            ws_variants = [a_w_s[j], jnp.tile(a_w_s[j][:, :ls, :ls], (1, reps, reps))]
            bs_variants = [a_b_s[j].T, jnp.tile(a_b_s[j][:, :ls], (1, reps)).T]
            for gi, grp in enumerate(groups):
                xf, xb, nseq, slen = grp
                u, v = _a_in(xb, w_in)
                gated, vn = _spatial(u, v, ln_g, ln_b, ws_variants[gi], bs_variants[gi],
                                     period=min(slen, blk), want_vn=gi == 1)
                if gi == 1:
                    v_rows.append(vn.reshape(nseq, slen, -1))
                grp[0], grp[1] = _mm_res_ln(gated, w_out, xf, row(ln1_g), row(ln1_b), alpha, head_major=False)
        else:
            w_in = b_w_in[j].astype(BF16)
            w_out = b_w_out[j].astype(BF16)
            gn = b_gn_g[j].reshape(H, 1, LANES)
            for gi, grp in enumerate(groups):
                xf, xb, nseq, slen = grp
                q, k, lf, v, sg = _hgrn_in(xb, w_in, b_lb, i)
                o, s_fin = _gla(q, k, lf, v, sg, gn, nseq, slen, state_hgrn[j] if gi == 1 else None)
                (s_sample if gi == 1 else s_prompt).append(s_fin)
                grp[0], grp[1] = _mm_res_ln(o, w_out, xf, row(ln1_g), row(ln1_b), alpha, head_major=True)

        pad_cols = lambda a: jnp.pad(a, ((0, 0), (0, f_pad)))
        wa = pad_cols(f_w_up[i][:, :d_ff].astype(BF16))
        wu = pad_cols(f_w_up[i][:, d_ff:].astype(BF16))
        cw = pad_cols(f_conv_w[i])
        cb = pad_cols(f_conv_b[i][None, :])
        wd = jnp.pad(f_w_down[i].astype(BF16), ((0, f_pad), (0, 0)))
        for gi, grp in enumerate(groups):
            xf, xb, nseq, slen = grp
            state = None
            if gi == 1:
                state = jnp.pad(cache_conv[i], ((0, 0), (SUBLANES - 2, 0), (0, f_pad)))
            gated, tail = _ffn_up(xb, wa, wu, cw, cb, nseq, slen, state)
            (c_sample if gi == 1 else c_prompt).append(tail[:, SUBLANES - 2:, :d_ff])
            grp[0], grp[1] = _mm_res_ln(gated, wd, xf, row(ln2_g), row(ln2_b), alpha, head_major=False)

    return (groups[0][0].reshape(bp, lp, D), groups[1][0].reshape(bs, ls, D), jnp.stack(v_rows),
            jnp.stack(s_prompt), jnp.stack(s_sample), jnp.stack(c_prompt), jnp.stack(c_sample))
```

```python
import functools

import jax
import jax.numpy as jnp
from jax import lax
from jax.experimental import pallas as pl
from jax.experimental.pallas import tpu as pltpu

F32 = jnp.float32
BF16 = jnp.bfloat16

LN_EPS = 1e-5
RMS_EPS = 1e-6
LANES = 128
SUBLANES = 8
VMEM_LIMIT_BYTES = 56 * 1024 * 1024
NEG_BIG = -1e30


def _cparams(*sem):
    return pltpu.CompilerParams(dimension_semantics=sem, vmem_limit_bytes=VMEM_LIMIT_BYTES)


def _tile(n, pref):
    t = min(n, pref)
    while n % t:
        t -= LANES
    return t


def _dot(a, b):
    return jnp.dot(a, b, preferred_element_type=F32)


def _log2(n):
    assert n > 0 and n & (n - 1) == 0, f"{n} is not a power of two"
    return n.bit_length() - 1


def _block_id(idx, size):
    return lax.shift_right_logical(idx, _log2(size))


def _layer_norm_rows(x, g, b):
    mu = jnp.mean(x, axis=-1, keepdims=True)
    xc = x - mu
    var = jnp.mean(xc * xc, axis=-1, keepdims=True)
    return xc * lax.rsqrt(var + LN_EPS) * g + b


def _a_in_kernel(x_ref, wu_ref, wv_ref, u_ref, v_ref):
    x = x_ref[...]
    u_ref[...] = jax.nn.gelu(_dot(x, wu_ref[...])).astype(u_ref.dtype)
    v_ref[...] = jax.nn.gelu(_dot(x, wv_ref[...]))


def _a_in(xb, w_in):
    T, D = xb.shape
    W = w_in.shape[1] // 2
    tm, tn = _tile(T, 1024), _tile(W, 512)
    nn = W // tn
    return pl.pallas_call(
        _a_in_kernel,
        grid=(T // tm, nn),
        in_specs=[pl.BlockSpec((tm, D), lambda m, n: (m, 0)),
                  pl.BlockSpec((D, tn), lambda m, n: (0, n)),
                  pl.BlockSpec((D, tn), lambda m, n: (0, n + nn))],
        out_specs=[pl.BlockSpec((tm, tn), lambda m, n: (m, n)),
                   pl.BlockSpec((tm, tn), lambda m, n: (m, n))],
        out_shape=[jax.ShapeDtypeStruct((T, W), BF16), jax.ShapeDtypeStruct((T, W), F32)],
        compiler_params=_cparams("parallel", "arbitrary"),
        name="a_in",
    )(xb, w_in, w_in)


def _spatial_kernel(u_ref, v_ref, g_ref, b_ref, ws_ref, bs_ref, o_ref, vn_ref, *, nb, groups, period):
    vn_ref[...] = _layer_norm_rows(v_ref[...], g_ref[...], b_ref[...])
    row = lax.broadcasted_iota(jnp.int32, (LANES, LANES), 0)
    col = lax.broadcasted_iota(jnp.int32, (LANES, LANES), 1)
    keep = (col <= row) & (_block_id(row, period) == _block_id(col, period))
    for g in range(groups):
        cs = slice(g * LANES, (g + 1) * LANES)
        wsg = jnp.where(keep, ws_ref[g], 0.0).astype(BF16)
        bias = bs_ref[:, g:g + 1]
        for j in range(nb):
            rs = slice(j * LANES, (j + 1) * LANES)
            mixed = _dot(wsg, vn_ref[rs, cs].astype(BF16)) + bias
            o_ref[rs, cs] = (u_ref[rs, cs].astype(F32) * mixed).astype(o_ref.dtype)


def _spatial(u, v, ln_g, ln_b, ws, bs_t, period, want_vn):
    T, W = v.shape
    G = ws.shape[0]
    tb = _tile(T, 2 * LANES)
    row_spec = pl.BlockSpec((tb, W), lambda i: (i, 0))
    vec_spec = pl.BlockSpec((1, W), lambda i: (0, 0))
    kern = functools.partial(_spatial_kernel, nb=tb // LANES, groups=G, period=period)
    common = dict(
        grid=(T // tb,),
        in_specs=[row_spec, row_spec, vec_spec, vec_spec,
                  pl.BlockSpec((G, LANES, LANES), lambda i: (0, 0, 0)),
                  pl.BlockSpec((LANES, G), lambda i: (0, 0))],
        compiler_params=_cparams("parallel"),
    )
    if want_vn:
        return pl.pallas_call(
            kern, out_specs=[row_spec, row_spec],
            out_shape=[jax.ShapeDtypeStruct((T, W), BF16), jax.ShapeDtypeStruct((T, W), F32)],
            name="spatial_vn", **common)(u, v, ln_g, ln_b, ws, bs_t)
    gated = pl.pallas_call(
        kern, out_specs=row_spec, out_shape=jax.ShapeDtypeStruct((T, W), BF16),
        scratch_shapes=[pltpu.VMEM((tb, W), F32)],
        name="spatial", **common)(u, v, ln_g, ln_b, ws, bs_t)
    return gated, None


LN_ROWS = 64
LN_COLS = 1024
MM_K_TILE = 2816


def _ln_inplace(src_ref, g_ref, b_ref, of_ref, ob_ref, rows):
    n = src_ref.shape[1]
    cw = _tile(n, LN_COLS)
    slabs = [slice(j * cw, (j + 1) * cw) for j in range(n // cw)]

    def body(i, c):
        rs = pl.ds(pl.multiple_of(i * LN_ROWS, LN_ROWS), LN_ROWS)
        mu = sum(jnp.sum(src_ref[rs, cs], axis=-1, keepdims=True) for cs in slabs) * (1.0 / n)
        var = sum(jnp.sum(jnp.square(src_ref[rs, cs] - mu), axis=-1, keepdims=True) for cs in slabs) * (1.0 / n)
        r = lax.rsqrt(var + LN_EPS)
        for cs in slabs:
            y = (src_ref[rs, cs] - mu) * r * g_ref[:, cs] + b_ref[:, cs]
            of_ref[rs, cs] = y
            ob_ref[rs, cs] = y.astype(ob_ref.dtype)
        return c

    lax.fori_loop(0, rows // LN_ROWS, body, 0)


def _mm_res_ln_kernel(x_ref, w_ref, res_ref, g_ref, b_ref, of_ref, ob_ref, *scratch,
                      alpha, nk, nn, tn, heads):
    k = pl.program_id(1)
    n = pl.program_id(2)
    if heads:
        xs_ref, = scratch

        @pl.when(n == 0)
        def _():
            for h in range(heads):
                xs_ref[:, h * LANES:(h + 1) * LANES] = x_ref[h]

        x = xs_ref[...]
    else:
        x = x_ref[...]
    cols = pl.ds(pl.multiple_of(n * tn, tn), tn)

    @pl.when(k == 0)
    def _():
        of_ref[:, cols] = alpha * res_ref[...]

    of_ref[:, cols] += _dot(x, w_ref[...])

    @pl.when((k == nk - 1) & (n == nn - 1))
    def _():
        _ln_inplace(of_ref, g_ref, b_ref, of_ref, ob_ref, of_ref.shape[0])


def _mm_res_ln(x, w, res, ln_g, ln_b, alpha, head_major):
    T, N = res.shape
    K = w.shape[0]
    tm, tn = _tile(T, 1024), _tile(N, 512)
    tk = K if K <= 4096 else _tile(K, MM_K_TILE)
    nk, nn = K // tk, N // tn
    heads = K // LANES if head_major else 0
    once = dict(pipeline_mode=pl.Buffered(1))
    if head_major:
        assert nk == 1
        x_spec = pl.BlockSpec((heads, tm, LANES), lambda m, k, n: (0, m, 0), **once)
        scratch = [pltpu.VMEM((tm, K), BF16)]
    elif nk == 1:
        x_spec = pl.BlockSpec((tm, K), lambda m, k, n: (m, 0), **once)
        scratch = []
    else:
        x_spec = pl.BlockSpec((tm, tk), lambda m, k, n: (m, k))
        scratch = []
    row_spec = pl.BlockSpec((tm, N), lambda m, k, n: (m, 0), **once)
    vec_spec = pl.BlockSpec((1, N), lambda m, k, n: (0, 0))
    res_spec = pl.BlockSpec((tm, tn), lambda m, k, n: (m, jnp.where(k == 0, n, nn - 1)))
    return pl.pallas_call(
        functools.partial(_mm_res_ln_kernel, alpha=alpha, nk=nk, nn=nn, tn=tn, heads=heads),
        grid=(T // tm, nk, nn),
        in_specs=[x_spec,
                  pl.BlockSpec((tk, tn), lambda m, k, n: (k, n)),
                  res_spec, vec_spec, vec_spec],
        out_specs=[row_spec, row_spec],
        out_shape=[jax.ShapeDtypeStruct((T, N), F32), jax.ShapeDtypeStruct((T, N), BF16)],
        scratch_shapes=scratch,
        compiler_params=_cparams("parallel", "arbitrary", "arbitrary"),
        name="mm_res_ln",
    )(x, w, res, ln_g, ln_b)


FFN_SUB_COLS = 256


def _ffn_up_kernel(x_ref, wa_ref, wu_ref, cw_ref, cb_ref, *rest, nseq, slen, tiles_per_seq, carry):
    if carry:
        g_ref, cs_ref, halo_ref = rest
    else:
        st_ref, g_ref, cs_ref = rest
    m = pl.program_id(0)
    f = pl.program_id(1)
    tm, tf = g_ref.shape
    sub = _tile(tf, FFN_SUB_COLS)

    if carry:
        @pl.when(m % tiles_per_seq == 0)
        def _():
            halo_ref[f] = jnp.zeros((SUBLANES, tf), F32)

    _log2(slen)
    pos = lax.broadcasted_iota(jnp.int32, (tm, 1), 0) & (slen - 1)
    x = x_ref[...]
    for j in range(tf // sub):
        cs = slice(j * sub, (j + 1) * sub)
        a = _dot(x, wa_ref[:, cs])
        u = _dot(x, wu_ref[:, cs])
        if carry:
            st = halo_ref[f, :, cs][None]
        else:
            st = st_ref[:, :, cs]

        def expand(rows):
            return jnp.broadcast_to(rows, (nseq, slen, sub)).reshape(tm, sub)

        s_m1 = expand(st[:, SUBLANES - 1:SUBLANES, :])
        s_m2 = expand(st[:, SUBLANES - 2:SUBLANES - 1, :])
        prev1 = jnp.where(pos == 0, s_m1, pltpu.roll(a, 1, 0))
        prev2 = jnp.where(pos == 0, s_m2, jnp.where(pos == 1, s_m1, pltpu.roll(a, 2, 0)))
        cw = cw_ref[:, cs]
        c = cb_ref[:, cs] + prev2 * cw[0:1, :] + prev1 * cw[1:2, :] + a * cw[2:3, :]

        tail = a.reshape(nseq, slen, sub)[:, slen - SUBLANES:, :]
        cs_ref[:, :, cs] = tail
        if carry:
            halo_ref[f, :, cs] = tail[0]
        g_ref[:, cs] = (jax.nn.silu(c) * u).astype(g_ref.dtype)


def _ffn_up(xb, w_up, cw, cb, nseq, slen, state):
    T, D = xb.shape
    F = w_up.shape[1] // 2
    tf = _tile(F, 512)
    nf = F // tf
    carry = state is None
    if carry:
        tm = _tile(slen, 1024)
        tiles_per_seq = slen // tm
        seq_per_tile = 1
        cs_spec = pl.BlockSpec((1, SUBLANES, tf), lambda m, f: (m, 0, f))
        extra_in, extra_specs = [], []
        scratch = [pltpu.VMEM((nf, SUBLANES, tf), F32)]
    else:
        tm = _tile(T, 1024)
        assert tm % slen == 0
        tiles_per_seq = 1
        seq_per_tile = tm // slen
        cs_spec = pl.BlockSpec((seq_per_tile, SUBLANES, tf), lambda m, f: (m, 0, f))
        extra_in, extra_specs = [state], [cs_spec]
        scratch = []
    kern = functools.partial(_ffn_up_kernel, nseq=seq_per_tile, slen=tm // seq_per_tile,
                             tiles_per_seq=tiles_per_seq, carry=carry)
    n_tails = (T // tm) * seq_per_tile
    gated, tails = pl.pallas_call(
        kern,
        grid=(T // tm, nf),
        in_specs=[pl.BlockSpec((tm, D), lambda m, f: (m, 0)),
                  pl.BlockSpec((D, tf), lambda m, f: (0, f)),
                  pl.BlockSpec((D, tf), lambda m, f: (0, f + nf)),
                  pl.BlockSpec((3, tf), lambda m, f: (0, f)),
                  pl.BlockSpec((1, tf), lambda m, f: (0, f))] + extra_specs,
        out_specs=[pl.BlockSpec((tm, tf), lambda m, f: (m, f)), cs_spec],
        out_shape=[jax.ShapeDtypeStruct((T, F), BF16), jax.ShapeDtypeStruct((n_tails, SUBLANES, F), F32)],
        scratch_shapes=scratch,
        compiler_params=_cparams("arbitrary", "arbitrary"),
        name="ffn_up_carry" if carry else "ffn_up_state",
    )(xb, w_up, w_up, cw, cb, *extra_in)
    return gated, tails[tiles_per_seq - 1::tiles_per_seq]


def _hgrn_in_kernel(x_ref, wq_ref, wf_ref, wi_ref, wg_ref, lb_ref,
                    q_ref, k_ref, lf_ref, v_ref, sg_ref, *, layer, hh):
    x = x_ref[...]
    lbv = lb_ref[...]
    e = jnp.exp(lbv - jnp.max(lbv, axis=0, keepdims=True))
    p = e / jnp.sum(e, axis=0, keepdims=True)
    lb = jnp.sum(p[1:layer + 1, :], axis=0, keepdims=True)

    def put(ref, val):
        for j in range(hh):
            ref[j] = val[:, j * LANES:(j + 1) * LANES].astype(ref.dtype)

    put(q_ref, _dot(x, wq_ref[...]))
    f = lb + (1.0 - lb) * jax.nn.sigmoid(_dot(x, wf_ref[...]))
    put(lf_ref, jnp.log(f))
    put(k_ref, 1.0 - f)
    put(v_ref, _dot(x, wi_ref[...]))
    put(sg_ref, jax.nn.silu(_dot(x, wg_ref[...])))


def _hgrn_in(xb, w_in, b_lb, layer):
    T, D = xb.shape
    QK = w_in.shape[1] // 4
    H = QK // LANES
    tm, tn = _tile(T, 1024), _tile(QK, 256)
    nn = QK // tn
    hh = tn // LANES
    depth = b_lb.shape[0]
    w_specs = [pl.BlockSpec((D, tn), lambda m, n, k=k: (0, n + k * nn)) for k in range(4)]
    o_spec = pl.BlockSpec((hh, tm, LANES), lambda m, n: (n, m, 0))
    hm = lambda dt: jax.ShapeDtypeStruct((H, T, LANES), dt)
    return pl.pallas_call(
        functools.partial(_hgrn_in_kernel, layer=layer, hh=hh),
        grid=(T // tm, nn),
        in_specs=[pl.BlockSpec((tm, D), lambda m, n: (m, 0))] + w_specs
                 + [pl.BlockSpec((depth, tn), lambda m, n: (0, n))],
        out_specs=[o_spec] * 5,
        out_shape=[hm(BF16), hm(BF16), hm(F32), hm(BF16), hm(BF16)],
        compiler_params=_cparams("parallel", "arbitrary"),
        name="hgrn_in",
    )(xb, w_in, w_in, w_in, w_in, b_lb)


GLA_BLOCK = 32
GLA_MAX_SPREAD = 60.0
GLA_HEADS_PER_ITER = 4

def _nt_dot(a, b):
    return lax.dot_general(a, b, (((1,), (1,)), ((), ())), preferred_element_type=F32)


def _tn_dot(a, b):
    return lax.dot_general(a, b, (((0,), (0,)), ((), ())), preferred_element_type=F32)


def _gla_kernel(q_ref, k_ref, lf_ref, v_ref, sg_ref, gn_ref, *rest, C, hh, has_init):
    if has_init:
        s0_ref, o_ref, sout_ref, st_ref, acc_ref = rest
    else:
        o_ref, sout_ref, st_ref, acc_ref = rest
    t = pl.program_id(2)

    @pl.when(t == 0)
    def _():
        if has_init:
            def init(h, c):
                st_ref[h] = s0_ref[0, h].T
                return c
            lax.fori_loop(0, hh, init, 0)
        else:
            st_ref[...] = jnp.zeros(st_ref.shape, F32)

    row = lax.broadcasted_iota(jnp.int32, (C, C), 0)
    col = lax.broadcasted_iota(jnp.int32, (C, C), 1)
    tri = jnp.where(col <= row, 1.0, 0.0).astype(BF16)
    n8 = C // SUBLANES
    sub = lax.broadcasted_iota(jnp.int32, (1, SUBLANES, 1), 1)
    half = C // 2
    hrow = lax.broadcasted_iota(jnp.int32, (half, half), 0)
    hcol = lax.broadcasted_iota(jnp.int32, (half, half), 1)
    w0 = min(GLA_BLOCK, C)
    nb0 = C // w0
    dr = min(C, LANES)
    drow = lax.broadcasted_iota(jnp.int32, (dr, dr), 0)
    dcol = lax.broadcasted_iota(jnp.int32, (dr, dr), 1)

    def widths(lo, hi):
        out = []
        while lo < hi:
            out.append(lo)
            lo *= 2
        return out

    def halves(x, w, which):
        return jnp.concatenate([x[(2 * p + which) * w:(2 * p + which + 1) * w] for p in range(C // (2 * w))], axis=0)

    def rows_at(x, first, step, reps):
        return jnp.concatenate(
            [jnp.broadcast_to(x[i:i + 1, :], (reps, LANES)) for i in range(first, C, step)], axis=0)

    def load(h):
        q = q_ref[h].astype(F32)
        k = k_ref[h].astype(F32)
        vb = v_ref[h]
        g = lf_ref[h]
        g1 = g.astype(BF16)
        r1 = g - g1.astype(F32)
        g2 = r1.astype(BF16)
        g3 = (r1 - g2.astype(F32)).astype(BF16)
        b = _dot(tri, g1) + _dot(tri, g2) + _dot(tri, g3)
        return q, k, vb, vb.astype(F32), b

    def level(w, q, k, v32, b):
        bref = rows_at(b, w - 1, 2 * w, w)
        qt = (halves(q, w, 1) * jnp.exp(halves(b, w, 1) - bref)).astype(BF16)
        ks = (halves(k, w, 0) * jnp.exp(bref - halves(b, w, 0))).astype(BF16)
        att = _nt_dot(qt, ks)
        if C > 2 * w:
            att = jnp.where(_block_id(hrow, w) == _block_id(hcol, w), att, 0.0)
        return _dot(att.astype(BF16), halves(v32, w, 0).astype(BF16))

    def finish(h, o):
        o = o * lax.rsqrt(jnp.mean(o * o, axis=-1, keepdims=True) + RMS_EPS)
        o_ref[h] = (o * gn_ref[h] * sg_ref[h].astype(F32)).astype(o_ref.dtype)

    def common(h, slot):
        q, k, vb, v32, b = load(h)
        b_last = b[C - 1:C, :]
        st = st_ref[h]
        o_state = _nt_dot((q * jnp.exp(b)).astype(BF16), st.astype(BF16))
        acc_ref[slot] = o_state
        kd = (k * jnp.exp(b_last - b)).astype(BF16)
        st_new = st * jnp.exp(b_last) + _tn_dot(vb, kd)
        st_ref[h] = st_new
        sout_ref[0, h] = st_new.T

        blocks = [o_state[i * w0:(i + 1) * w0] for i in range(nb0)]
        for w in widths(w0, C):
            ot = level(w, q, k, v32, b)
            for p in range(C // (2 * w)):
                for i in range(w // w0):
                    blocks[(2 * p + 1) * w // w0 + i] += ot[p * w + i * w0:p * w + (i + 1) * w0]
        bref = rows_at(b, 0, w0, w0)
        qe = (q * jnp.exp(b - bref)).astype(BF16)
        ke = (k * jnp.exp(jnp.minimum(bref - b, GLA_MAX_SPREAD))).astype(BF16)
        keep = (_block_id(drow, w0) == _block_id(dcol, w0)) & (dcol <= drow)
        for r in range(C // dr):
            rs = slice(r * dr, (r + 1) * dr)
            od = _dot(jnp.where(keep, _nt_dot(qe[rs], ke[rs]), 0.0).astype(BF16), vb[rs])
            for i in range(dr // w0):
                blocks[r * dr // w0 + i] += od[i * w0:(i + 1) * w0]
        finish(h, jnp.concatenate(blocks, axis=0))
        return jnp.max(rows_at(b, 0, w0, 1) - rows_at(b, w0 - 1, w0, 1))

    def exact(h, slot):
        q, k, _, v32, b = load(h)
        for w in widths(SUBLANES, C):
            ot = level(w, q, k, v32, b)
            for p in range(C // (2 * w)):
                acc_ref[slot, (2 * p + 1) * w:(2 * p + 2) * w, :] += ot[p * w:(p + 1) * w, :]
        b3 = b.reshape(n8, SUBLANES, LANES)
        q3 = q.reshape(n8, SUBLANES, LANES)
        k3 = k.reshape(n8, SUBLANES, LANES)
        v3 = v32.reshape(n8, SUBLANES, LANES)
        o3 = jnp.zeros((n8, SUBLANES, LANES), F32)
        for s in range(SUBLANES):
            d = jnp.where(sub >= s, b3 - b3[:, s:s + 1, :], NEG_BIG)
            e = q3 * jnp.exp(d) * k3[:, s:s + 1, :]
            o3 = o3 + jnp.sum(e, axis=-1, keepdims=True) * v3[:, s:s + 1, :]
        finish(h, acc_ref[slot] + o3.reshape(C, LANES))

    def heads(i, c):
        hs = [i * GLA_HEADS_PER_ITER + j for j in range(GLA_HEADS_PER_ITER)]
        spreads = [common(h, slot) for slot, h in enumerate(hs)]
        for slot, (h, spread) in enumerate(zip(hs, spreads)):
            @pl.when(jnp.logical_not(spread <= GLA_MAX_SPREAD))
            def _(h=h, slot=slot):
                exact(h, slot)
        return c

    lax.fori_loop(0, hh // GLA_HEADS_PER_ITER, heads, 0)


def _gla(q, k, lf, v, sg, gn, nseq, slen, s0):
    H, T, _ = q.shape
    C = _tile(slen, 256)
    tiles = slen // C
    hh = _tile(H * LANES, 16 * LANES) // LANES
    row_spec = pl.BlockSpec((hh, C, LANES), lambda s, g, t: (g, s * tiles + t, 0))
    st_spec = pl.BlockSpec((1, hh, LANES, LANES), lambda s, g, t: (s, g, 0, 0))
    in_specs = [row_spec] * 5 + [pl.BlockSpec((hh, 1, LANES), lambda s, g, t: (g, 0, 0))]
    args = [q, k, lf, v, sg, gn]
    if s0 is not None:
        in_specs.append(st_spec)
        args.append(s0)
    return pl.pallas_call(
        functools.partial(_gla_kernel, C=C, hh=hh, has_init=s0 is not None),
        grid=(nseq, H // hh, tiles),
        in_specs=in_specs,
        out_specs=[row_spec, st_spec],
        out_shape=[jax.ShapeDtypeStruct((H, T, LANES), BF16),
                   jax.ShapeDtypeStruct((nseq, H, LANES, LANES), F32)],
        scratch_shapes=[pltpu.VMEM((hh, LANES, LANES), F32), pltpu.VMEM((GLA_HEADS_PER_ITER, C, LANES), F32)],
        compiler_params=_cparams("parallel", "parallel", "arbitrary"),
        name="gla_init" if s0 is not None else "gla",
    )(*args)


FF_ALIGN = 512


def kernel(x_prompt, x_sample, state_hgrn, cache_conv, a_w_in, a_ln_g, a_ln_b, a_w_s, a_b_s, a_w_out,
           b_w_in, b_lb, b_gn_g, b_w_out, f_w_up, f_conv_w, f_conv_b, f_w_down, ln1_g, ln1_b, ln2_g, ln2_b):
    depth = ln1_g.shape[0]
    alpha = (2 * depth) ** 0.25
    bp, lp, D = x_prompt.shape
    bs, ls, _ = x_sample.shape
    blk = a_w_s.shape[-1]
    assert blk == LANES and lp % blk == 0 and blk % ls == 0 and (bs * ls) % blk == 0
    d_ff = f_conv_w.shape[-1]
    f_pad = -d_ff % FF_ALIGN
    H = b_gn_g.shape[-1] // LANES

    xp = x_prompt.reshape(bp * lp, D)
    xs = x_sample.reshape(bs * ls, D)
    groups = [[xp, xp.astype(BF16), bp, lp], [xs, xs.astype(BF16), bs, ls]]

    v_rows, s_prompt, s_sample, c_prompt, c_sample = [], [], [], [], []
    for i in range(depth):
        j = i // 2
        row = lambda a: a[i][None, :]
        if i % 2 == 0:
            w_in = a_w_in[j].astype(BF16)
            w_out = a_w_out[j].astype(BF16)
            ln_g, ln_b = a_ln_g[j][None, :], a_ln_b[j][None, :]
            reps = blk // ls
            ws_variants = [a_w_s[j], jnp.tile(a_w_s[j][:, :ls, :ls], (1, reps, reps))]
            bs_variants = [a_b_s[j].T, jnp.tile(a_b_s[j][:, :ls], (1, reps)).T]
            for gi, grp in enumerate(groups):
                xf, xb, nseq, slen = grp
                u, v = _a_in(xb, w_in)
                gated, vn = _spatial(u, v, ln_g, ln_b, ws_variants[gi], bs_variants[gi],
                                     period=min(slen, blk), want_vn=gi == 1)
                if gi == 1:
                    v_rows.append(vn.reshape(nseq, slen, -1))
                grp[0], grp[1] = _mm_res_ln(gated, w_out, xf, row(ln1_g), row(ln1_b), alpha, head_major=False)
        else:
            w_in = b_w_in[j].astype(BF16)
            w_out = b_w_out[j].astype(BF16)
            gn = b_gn_g[j].reshape(H, 1, LANES)
            for gi, grp in enumerate(groups):
                xf, xb, nseq, slen = grp
                q, k, lf, v, sg = _hgrn_in(xb, w_in, b_lb, i)
                o, s_fin = _gla(q, k, lf, v, sg, gn, nseq, slen, state_hgrn[j] if gi == 1 else None)
                (s_sample if gi == 1 else s_prompt).append(s_fin)
                grp[0], grp[1] = _mm_res_ln(o, w_out, xf, row(ln1_g), row(ln1_b), alpha, head_major=True)

        pad_cols = lambda a: jnp.pad(a, ((0, 0), (0, f_pad)))
        w_up = jnp.pad(f_w_up[i].reshape(D, 2, d_ff), ((0, 0), (0, 0), (0, f_pad))).astype(BF16)
        w_up = w_up.reshape(D, 2 * (d_ff + f_pad))
        cw = pad_cols(f_conv_w[i])
        cb = pad_cols(f_conv_b[i][None, :])
        wd = jnp.pad(f_w_down[i].astype(BF16), ((0, f_pad), (0, 0)))
        for gi, grp in enumerate(groups):
            xf, xb, nseq, slen = grp
            state = None
            if gi == 1:
                state = jnp.pad(cache_conv[i], ((0, 0), (SUBLANES - 2, 0), (0, f_pad)))
            gated, tail = _ffn_up(xb, w_up, cw, cb, nseq, slen, state)
            (c_sample if gi == 1 else c_prompt).append(tail[:, SUBLANES - 2:, :d_ff])
            grp[0], grp[1] = _mm_res_ln(gated, wd, xf, row(ln2_g), row(ln2_b), alpha, head_major=False)

    return (groups[0][0].reshape(bp, lp, D), groups[1][0].reshape(bs, ls, D), jnp.stack(v_rows),
            jnp.stack(s_prompt), jnp.stack(s_sample), jnp.stack(c_prompt), jnp.stack(c_sample))
```

```python
import functools

import jax
import jax.numpy as jnp
from jax import lax
from jax.experimental import pallas as pl
from jax.experimental.pallas import tpu as pltpu

F32 = jnp.float32
BF16 = jnp.bfloat16

LN_EPS = 1e-5
RMS_EPS = 1e-6
LANES = 128
SUBLANES = 8
VMEM_LIMIT_BYTES = 56 * 1024 * 1024
NEG_BIG = -1e30


def _cparams(*sem):
    return pltpu.CompilerParams(dimension_semantics=sem, vmem_limit_bytes=VMEM_LIMIT_BYTES)


def _tile(n, pref):
    t = min(n, pref)
    while n % t:
        t -= LANES
    return t


def _dot(a, b):
    return jnp.dot(a, b, preferred_element_type=F32)


def _log2(n):
    assert n > 0 and n & (n - 1) == 0, f"{n} is not a power of two"
    return n.bit_length() - 1


def _block_id(idx, size):
    return lax.shift_right_logical(idx, _log2(size))


def _layer_norm_rows(x, g, b):
    mu = jnp.mean(x, axis=-1, keepdims=True)
    xc = x - mu
    var = jnp.mean(xc * xc, axis=-1, keepdims=True)
    return xc * lax.rsqrt(var + LN_EPS) * g + b


def _a_in_kernel(x_ref, wu_ref, wv_ref, u_ref, v_ref):
    x = x_ref[...]
    u_ref[...] = jax.nn.gelu(_dot(x, wu_ref[...])).astype(u_ref.dtype)
    v_ref[...] = jax.nn.gelu(_dot(x, wv_ref[...]))


def _a_in(xb, w_in):
    T, D = xb.shape
    W = w_in.shape[1] // 2
    tm, tn = _tile(T, 1024), _tile(W, 512)
    nn = W // tn
    return pl.pallas_call(
        _a_in_kernel,
        grid=(T // tm, nn),
        in_specs=[pl.BlockSpec((tm, D), lambda m, n: (m, 0)),
                  pl.BlockSpec((D, tn), lambda m, n: (0, n)),
                  pl.BlockSpec((D, tn), lambda m, n: (0, n + nn))],
        out_specs=[pl.BlockSpec((tm, tn), lambda m, n: (m, n)),
                   pl.BlockSpec((tm, tn), lambda m, n: (m, n))],
        out_shape=[jax.ShapeDtypeStruct((T, W), BF16), jax.ShapeDtypeStruct((T, W), F32)],
        compiler_params=_cparams("parallel", "arbitrary"),
        name="a_in",
    )(xb, w_in, w_in)


def _spatial_kernel(u_ref, v_ref, g_ref, b_ref, ws_ref, bs_ref, o_ref, vn_ref, *, nb, groups, period):
    vn_ref[...] = _layer_norm_rows(v_ref[...], g_ref[...], b_ref[...])
    row = lax.broadcasted_iota(jnp.int32, (LANES, LANES), 0)
    col = lax.broadcasted_iota(jnp.int32, (LANES, LANES), 1)
    keep = (col <= row) & (_block_id(row, period) == _block_id(col, period))
    for g in range(groups):
        cs = slice(g * LANES, (g + 1) * LANES)
        wsg = jnp.where(keep, ws_ref[g], 0.0).astype(BF16)
        bias = bs_ref[:, g:g + 1]
        for j in range(nb):
            rs = slice(j * LANES, (j + 1) * LANES)
            mixed = _dot(wsg, vn_ref[rs, cs].astype(BF16)) + bias
            o_ref[rs, cs] = (u_ref[rs, cs].astype(F32) * mixed).astype(o_ref.dtype)


def _spatial(u, v, ln_g, ln_b, ws, bs_t, period, want_vn):
    T, W = v.shape
    G = ws.shape[0]
    tb = _tile(T, 2 * LANES)
    row_spec = pl.BlockSpec((tb, W), lambda i: (i, 0))
    vec_spec = pl.BlockSpec((1, W), lambda i: (0, 0))
    kern = functools.partial(_spatial_kernel, nb=tb // LANES, groups=G, period=period)
    common = dict(
        grid=(T // tb,),
        in_specs=[row_spec, row_spec, vec_spec, vec_spec,
                  pl.BlockSpec((G, LANES, LANES), lambda i: (0, 0, 0)),
                  pl.BlockSpec((LANES, G), lambda i: (0, 0))],
        compiler_params=_cparams("parallel"),
    )
    if want_vn:
        return pl.pallas_call(
            kern, out_specs=[row_spec, row_spec],
            out_shape=[jax.ShapeDtypeStruct((T, W), BF16), jax.ShapeDtypeStruct((T, W), F32)],
            name="spatial_vn", **common)(u, v, ln_g, ln_b, ws, bs_t)
    gated = pl.pallas_call(
        kern, out_specs=row_spec, out_shape=jax.ShapeDtypeStruct((T, W), BF16),
        scratch_shapes=[pltpu.VMEM((tb, W), F32)],
        name="spatial", **common)(u, v, ln_g, ln_b, ws, bs_t)
    return gated, None


LN_ROWS = 64
LN_COLS = 1024
MM_K_TILE = 2816


def _ln_inplace(src_ref, g_ref, b_ref, of_ref, ob_ref, rows):
    n = src_ref.shape[1]
    cw = _tile(n, LN_COLS)
    slabs = [slice(j * cw, (j + 1) * cw) for j in range(n // cw)]

    def body(i, c):
        rs = pl.ds(pl.multiple_of(i * LN_ROWS, LN_ROWS), LN_ROWS)
        mu = sum(jnp.sum(src_ref[rs, cs], axis=-1, keepdims=True) for cs in slabs) * (1.0 / n)
        var = sum(jnp.sum(jnp.square(src_ref[rs, cs] - mu), axis=-1, keepdims=True) for cs in slabs) * (1.0 / n)
        r = lax.rsqrt(var + LN_EPS)
        for cs in slabs:
            y = (src_ref[rs, cs] - mu) * r * g_ref[:, cs] + b_ref[:, cs]
            of_ref[rs, cs] = y
            ob_ref[rs, cs] = y.astype(ob_ref.dtype)
        return c

    lax.fori_loop(0, rows // LN_ROWS, body, 0)


def _mm_res_ln_kernel(x_ref, w_ref, res_ref, g_ref, b_ref, of_ref, ob_ref, *scratch,
                      alpha, nk, nn, tn, heads):
    k = pl.program_id(1)
    n = pl.program_id(2)
    if heads:
        xs_ref, = scratch

        @pl.when(n == 0)
        def _():
            for h in range(heads):
                xs_ref[:, h * LANES:(h + 1) * LANES] = x_ref[h]

        x = xs_ref[...]
    else:
        x = x_ref[...]
    cols = pl.ds(pl.multiple_of(n * tn, tn), tn)

    @pl.when(k == 0)
    def _():
        of_ref[:, cols] = alpha * res_ref[...]

    of_ref[:, cols] += _dot(x, w_ref[...])

    @pl.when((k == nk - 1) & (n == nn - 1))
    def _():
        _ln_inplace(of_ref, g_ref, b_ref, of_ref, ob_ref, of_ref.shape[0])


def _mm_res_ln(x, w, res, ln_g, ln_b, alpha, head_major):
    T, N = res.shape
    K = w.shape[0]
    tm, tn = _tile(T, 1024), _tile(N, 512)
    tk = K if K <= 4096 else _tile(K, MM_K_TILE)
    nk, nn = K // tk, N // tn
    heads = K // LANES if head_major else 0
    once = dict(pipeline_mode=pl.Buffered(1))
    if head_major:
        assert nk == 1
        x_spec = pl.BlockSpec((heads, tm, LANES), lambda m, k, n: (0, m, 0), **once)
        scratch = [pltpu.VMEM((tm, K), BF16)]
    elif nk == 1:
        x_spec = pl.BlockSpec((tm, K), lambda m, k, n: (m, 0), **once)
        scratch = []
    else:
        x_spec = pl.BlockSpec((tm, tk), lambda m, k, n: (m, k))
        scratch = []
    row_spec = pl.BlockSpec((tm, N), lambda m, k, n: (m, 0), **once)
    vec_spec = pl.BlockSpec((1, N), lambda m, k, n: (0, 0))
    res_spec = pl.BlockSpec((tm, tn), lambda m, k, n: (m, jnp.where(k == 0, n, nn - 1)))
    return pl.pallas_call(
        functools.partial(_mm_res_ln_kernel, alpha=alpha, nk=nk, nn=nn, tn=tn, heads=heads),
        grid=(T // tm, nk, nn),
        in_specs=[x_spec,
                  pl.BlockSpec((tk, tn), lambda m, k, n: (k, n)),
                  res_spec, vec_spec, vec_spec],
        out_specs=[row_spec, row_spec],
        out_shape=[jax.ShapeDtypeStruct((T, N), F32), jax.ShapeDtypeStruct((T, N), BF16)],
        scratch_shapes=scratch,
        compiler_params=_cparams("parallel", "arbitrary", "arbitrary"),
        name="mm_res_ln",
    )(x, w, res, ln_g, ln_b)


FFN_SUB_COLS = 256


def _ffn_up_kernel(x_ref, wa_ref, wu_ref, cw_ref, cb_ref, *rest, nseq, slen, tiles_per_seq, carry):
    if carry:
        g_ref, cs_ref, halo_ref = rest
    else:
        st_ref, g_ref, cs_ref = rest
    m = pl.program_id(0)
    f = pl.program_id(1)
    tm, tf = g_ref.shape
    sub = _tile(tf, FFN_SUB_COLS)

    if carry:
        @pl.when(m % tiles_per_seq == 0)
        def _():
            halo_ref[f] = jnp.zeros((SUBLANES, tf), F32)

    _log2(slen)
    pos = lax.broadcasted_iota(jnp.int32, (tm, 1), 0) & (slen - 1)
    x = x_ref[...]
    for j in range(tf // sub):
        cs = slice(j * sub, (j + 1) * sub)
        a = _dot(x, wa_ref[:, cs])
        u = _dot(x, wu_ref[:, cs])
        if carry:
            st = halo_ref[f, :, cs][None]
        else:
            st = st_ref[:, :, cs]

        def expand(rows):
            return jnp.broadcast_to(rows, (nseq, slen, sub)).reshape(tm, sub)

        s_m1 = expand(st[:, SUBLANES - 1:SUBLANES, :])
        s_m2 = expand(st[:, SUBLANES - 2:SUBLANES - 1, :])
        prev1 = jnp.where(pos == 0, s_m1, pltpu.roll(a, 1, 0))
        prev2 = jnp.where(pos == 0, s_m2, jnp.where(pos == 1, s_m1, pltpu.roll(a, 2, 0)))
        cw = cw_ref[:, cs]
        c = cb_ref[:, cs] + prev2 * cw[0:1, :] + prev1 * cw[1:2, :] + a * cw[2:3, :]

        tail = a.reshape(nseq, slen, sub)[:, slen - SUBLANES:, :]
        cs_ref[:, :, cs] = tail
        if carry:
            halo_ref[f, :, cs] = tail[0]
        g_ref[:, cs] = (jax.nn.silu(c) * u).astype(g_ref.dtype)


def _ffn_up(xb, w_up, cw, cb, nseq, slen, state):
    T, D = xb.shape
    F = w_up.shape[1] // 2
    tf = _tile(F, 512)
    nf = F // tf
    carry = state is None
    if carry:
        tm = _tile(slen, 1024)
        tiles_per_seq = slen // tm
        seq_per_tile = 1
        cs_spec = pl.BlockSpec((1, SUBLANES, tf), lambda m, f: (m, 0, f))
        extra_in, extra_specs = [], []
        scratch = [pltpu.VMEM((nf, SUBLANES, tf), F32)]
    else:
        tm = _tile(T, 1024)
        assert tm % slen == 0
        tiles_per_seq = 1
        seq_per_tile = tm // slen
        cs_spec = pl.BlockSpec((seq_per_tile, SUBLANES, tf), lambda m, f: (m, 0, f))
        extra_in, extra_specs = [state], [cs_spec]
        scratch = []
    kern = functools.partial(_ffn_up_kernel, nseq=seq_per_tile, slen=tm // seq_per_tile,
                             tiles_per_seq=tiles_per_seq, carry=carry)
    n_tails = (T // tm) * seq_per_tile
    gated, tails = pl.pallas_call(
        kern,
        grid=(T // tm, nf),
        in_specs=[pl.BlockSpec((tm, D), lambda m, f: (m, 0)),
                  pl.BlockSpec((D, tf), lambda m, f: (0, f)),
                  pl.BlockSpec((D, tf), lambda m, f: (0, f + nf)),
                  pl.BlockSpec((3, tf), lambda m, f: (0, f)),
                  pl.BlockSpec((1, tf), lambda m, f: (0, f))] + extra_specs,
        out_specs=[pl.BlockSpec((tm, tf), lambda m, f: (m, f)), cs_spec],
        out_shape=[jax.ShapeDtypeStruct((T, F), BF16), jax.ShapeDtypeStruct((n_tails, SUBLANES, F), F32)],
        scratch_shapes=scratch,
        compiler_params=_cparams("arbitrary", "arbitrary"),
        name="ffn_up_carry" if carry else "ffn_up_state",
    )(xb, w_up, w_up, cw, cb, *extra_in)
    return gated, tails[tiles_per_seq - 1::tiles_per_seq]


def _hgrn_in_kernel(x_ref, wq_ref, wf_ref, wi_ref, wg_ref, lb_ref,
                    q_ref, k_ref, lf_ref, v_ref, sg_ref, *, layer, hh):
    x = x_ref[...]
    lbv = lb_ref[...]
    e = jnp.exp(lbv - jnp.max(lbv, axis=0, keepdims=True))
    p = e / jnp.sum(e, axis=0, keepdims=True)
    lb = jnp.sum(p[1:layer + 1, :], axis=0, keepdims=True)

    def put(ref, val):
        for j in range(hh):
            ref[j] = val[:, j * LANES:(j + 1) * LANES].astype(ref.dtype)

    put(q_ref, _dot(x, wq_ref[...]))
    f = lb + (1.0 - lb) * jax.nn.sigmoid(_dot(x, wf_ref[...]))
    put(lf_ref, jnp.log(f))
    put(k_ref, 1.0 - f)
    put(v_ref, _dot(x, wi_ref[...]))
    put(sg_ref, jax.nn.silu(_dot(x, wg_ref[...])))


def _hgrn_in(xb, w_in, b_lb, layer):
    T, D = xb.shape
    QK = w_in.shape[1] // 4
    H = QK // LANES
    tm, tn = _tile(T, 1024), _tile(QK, 256)
    nn = QK // tn
    hh = tn // LANES
    depth = b_lb.shape[0]
    w_specs = [pl.BlockSpec((D, tn), lambda m, n, k=k: (0, n + k * nn)) for k in range(4)]
    o_spec = pl.BlockSpec((hh, tm, LANES), lambda m, n: (n, m, 0))
    hm = lambda dt: jax.ShapeDtypeStruct((H, T, LANES), dt)
    return pl.pallas_call(
        functools.partial(_hgrn_in_kernel, layer=layer, hh=hh),
        grid=(T // tm, nn),
        in_specs=[pl.BlockSpec((tm, D), lambda m, n: (m, 0))] + w_specs
                 + [pl.BlockSpec((depth, tn), lambda m, n: (0, n))],
        out_specs=[o_spec] * 5,
        out_shape=[hm(BF16), hm(BF16), hm(F32), hm(BF16), hm(BF16)],
        compiler_params=_cparams("parallel", "arbitrary"),
        name="hgrn_in",
    )(xb, w_in, w_in, w_in, w_in, b_lb)


GLA_BLOCK = 32
GLA_MAX_SPREAD = 60.0
GLA_HEADS_PER_ITER = 4

def _nt_dot(a, b):
    return lax.dot_general(a, b, (((1,), (1,)), ((), ())), preferred_element_type=F32)


def _tn_dot(a, b):
    return lax.dot_general(a, b, (((0,), (0,)), ((), ())), preferred_element_type=F32)


def _gla_kernel(q_ref, k_ref, lf_ref, v_ref, sg_ref, gn_ref, *rest, C, hh, has_init):
    if has_init:
        s0_ref, o_ref, sout_ref, st_ref, acc_ref = rest
    else:
        o_ref, sout_ref, st_ref, acc_ref = rest
    t = pl.program_id(2)

    @pl.when(t == 0)
    def _():
        if has_init:
            def init(h, c):
                st_ref[h] = s0_ref[0, h].T
                return c
            lax.fori_loop(0, hh, init, 0)
        else:
            st_ref[...] = jnp.zeros(st_ref.shape, F32)

    row = lax.broadcasted_iota(jnp.int32, (C, C), 0)
    col = lax.broadcasted_iota(jnp.int32, (C, C), 1)
    tri = jnp.where(col <= row, 1.0, 0.0).astype(BF16)
    n8 = C // SUBLANES
    sub = lax.broadcasted_iota(jnp.int32, (1, SUBLANES, 1), 1)
    half = C // 2
    hrow = lax.broadcasted_iota(jnp.int32, (half, half), 0)
    hcol = lax.broadcasted_iota(jnp.int32, (half, half), 1)
    w0 = min(GLA_BLOCK, C)
    nb0 = C // w0
    dr = min(C, LANES)
    drow = lax.broadcasted_iota(jnp.int32, (dr, dr), 0)
    dcol = lax.broadcasted_iota(jnp.int32, (dr, dr), 1)

    def widths(lo, hi):
        out = []
        while lo < hi:
            out.append(lo)
            lo *= 2
        return out

    def halves(x, w, which):
        return jnp.concatenate([x[(2 * p + which) * w:(2 * p + which + 1) * w] for p in range(C // (2 * w))], axis=0)

    def rows_at(x, first, step, reps):
        return jnp.concatenate(
            [jnp.broadcast_to(x[i:i + 1, :], (reps, LANES)) for i in range(first, C, step)], axis=0)

    def load(h):
        q = q_ref[h].astype(F32)
        k = k_ref[h].astype(F32)
        vb = v_ref[h]
        g = lf_ref[h]
        g1 = g.astype(BF16)
        r1 = g - g1.astype(F32)
        g2 = r1.astype(BF16)
        g3 = (r1 - g2.astype(F32)).astype(BF16)
        b = _dot(tri, g1) + _dot(tri, g2) + _dot(tri, g3)
        return q, k, vb, vb.astype(F32), b

    def level(w, q, k, v32, b):
        bref = rows_at(b, w - 1, 2 * w, w)
        qt = (halves(q, w, 1) * jnp.exp(halves(b, w, 1) - bref)).astype(BF16)
        ks = (halves(k, w, 0) * jnp.exp(bref - halves(b, w, 0))).astype(BF16)
        att = _nt_dot(qt, ks)
        if C > 2 * w:
            att = jnp.where(_block_id(hrow, w) == _block_id(hcol, w), att, 0.0)
        return _dot(att.astype(BF16), halves(v32, w, 0).astype(BF16))

    def finish(h, o):
        o = o * lax.rsqrt(jnp.mean(o * o, axis=-1, keepdims=True) + RMS_EPS)
        o_ref[h] = (o * gn_ref[h] * sg_ref[h].astype(F32)).astype(o_ref.dtype)

    def common(h, slot):
        q, k, vb, v32, b = load(h)
        b_last = b[C - 1:C, :]
        st = st_ref[h]
        o_state = _nt_dot((q * jnp.exp(b)).astype(BF16), st.astype(BF16))
        acc_ref[slot] = o_state
        kd = (k * jnp.exp(b_last - b)).astype(BF16)
        st_new = st * jnp.exp(b_last) + _tn_dot(vb, kd)
        st_ref[h] = st_new
        sout_ref[0, h] = st_new.T

        blocks = [o_state[i * w0:(i + 1) * w0] for i in range(nb0)]
        for w in widths(w0, C):
            ot = level(w, q, k, v32, b)
            for p in range(C // (2 * w)):
                for i in range(w // w0):
                    blocks[(2 * p + 1) * w // w0 + i] += ot[p * w + i * w0:p * w + (i + 1) * w0]
        bref = rows_at(b, 0, w0, w0)
        qe = (q * jnp.exp(b - bref)).astype(BF16)
        ke = (k * jnp.exp(jnp.minimum(bref - b, GLA_MAX_SPREAD))).astype(BF16)
        keep = (_block_id(drow, w0) == _block_id(dcol, w0)) & (dcol <= drow)
        for r in range(C // dr):
            rs = slice(r * dr, (r + 1) * dr)
            od = _dot(jnp.where(keep, _nt_dot(qe[rs], ke[rs]), 0.0).astype(BF16), vb[rs])
            for i in range(dr // w0):
                blocks[r * dr // w0 + i] += od[i * w0:(i + 1) * w0]
        finish(h, jnp.concatenate(blocks, axis=0))
        return jnp.max(rows_at(b, 0, w0, 1) - rows_at(b, w0 - 1, w0, 1))

    def exact(h, slot):
        q, k, _, v32, b = load(h)
        for w in widths(SUBLANES, C):
            ot = level(w, q, k, v32, b)
            for p in range(C // (2 * w)):
                acc_ref[slot, (2 * p + 1) * w:(2 * p + 2) * w, :] += ot[p * w:(p + 1) * w, :]
        b3 = b.reshape(n8, SUBLANES, LANES)
        q3 = q.reshape(n8, SUBLANES, LANES)
        k3 = k.reshape(n8, SUBLANES, LANES)
        v3 = v32.reshape(n8, SUBLANES, LANES)
        o3 = jnp.zeros((n8, SUBLANES, LANES), F32)
        for s in range(SUBLANES):
            d = jnp.where(sub >= s, b3 - b3[:, s:s + 1, :], NEG_BIG)
            e = q3 * jnp.exp(d) * k3[:, s:s + 1, :]
            o3 = o3 + jnp.sum(e, axis=-1, keepdims=True) * v3[:, s:s + 1, :]
        finish(h, acc_ref[slot] + o3.reshape(C, LANES))

    def heads(i, c):
        hs = [i * GLA_HEADS_PER_ITER + j for j in range(GLA_HEADS_PER_ITER)]
        spreads = [common(h, slot) for slot, h in enumerate(hs)]
        for slot, (h, spread) in enumerate(zip(hs, spreads)):
            @pl.when(jnp.logical_not(spread <= GLA_MAX_SPREAD))
            def _(h=h, slot=slot):
                exact(h, slot)
        return c

    lax.fori_loop(0, hh // GLA_HEADS_PER_ITER, heads, 0)


def _gla(q, k, lf, v, sg, gn, nseq, slen, s0):
    H, T, _ = q.shape
    C = _tile(slen, 256)
    tiles = slen // C
    hh = _tile(H * LANES, 16 * LANES) // LANES
    row_spec = pl.BlockSpec((hh, C, LANES), lambda s, g, t: (g, s * tiles + t, 0))
    st_spec = pl.BlockSpec((1, hh, LANES, LANES), lambda s, g, t: (s, g, 0, 0))
    in_specs = [row_spec] * 5 + [pl.BlockSpec((hh, 1, LANES), lambda s, g, t: (g, 0, 0))]
    args = [q, k, lf, v, sg, gn]
    if s0 is not None:
        in_specs.append(st_spec)
        args.append(s0)
    return pl.pallas_call(
        functools.partial(_gla_kernel, C=C, hh=hh, has_init=s0 is not None),
        grid=(nseq, H // hh, tiles),
        in_specs=in_specs,
        out_specs=[row_spec, st_spec],
        out_shape=[jax.ShapeDtypeStruct((H, T, LANES), BF16),
                   jax.ShapeDtypeStruct((nseq, H, LANES, LANES), F32)],
        scratch_shapes=[pltpu.VMEM((hh, LANES, LANES), F32), pltpu.VMEM((GLA_HEADS_PER_ITER, C, LANES), F32)],
        compiler_params=_cparams("parallel", "parallel", "arbitrary"),
        name="gla_init" if s0 is not None else "gla",
    )(*args)


PREP_ROWS = 64
PREP_BLOCK = 256


def _cast_pad_halves_kernel(w_ref, o_ref):
    n = w_ref.shape[1]
    o_ref[:, :n] = w_ref[...].astype(o_ref.dtype)
    if o_ref.shape[1] > n:
        o_ref[:, n:] = jnp.zeros((o_ref.shape[0], o_ref.shape[1] - n), o_ref.dtype)


def _cast_pad_halves(w, pad):
    D, F2 = w.shape
    F = F2 // 2
    tr = _tile(D, PREP_ROWS)
    return pl.pallas_call(
        _cast_pad_halves_kernel,
        grid=(D // tr, 2),
        in_specs=[pl.BlockSpec((tr, F), lambda r, h: (r, h))],
        out_specs=pl.BlockSpec((tr, F + pad), lambda r, h: (r, h)),
        out_shape=jax.ShapeDtypeStruct((D, 2 * (F + pad)), BF16),
        compiler_params=_cparams("parallel", "parallel"),
        name="cast_pad_halves",
    )(w)


def _cast_pad_rows_kernel(w_ref, o_ref, *, nb):
    r = pl.program_id(0)
    o_ref[...] = jnp.where(r < nb, w_ref[...], 0.0).astype(o_ref.dtype)


def _cast_pad_rows(w, pad):
    F, D = w.shape
    rb = PREP_BLOCK
    assert F % rb == 0 and pad % rb == 0
    nb = F // rb
    return pl.pallas_call(
        functools.partial(_cast_pad_rows_kernel, nb=nb),
        grid=((F + pad) // rb,),
        in_specs=[pl.BlockSpec((rb, D), lambda r: (jnp.minimum(r, nb - 1), 0))],
        out_specs=pl.BlockSpec((rb, D), lambda r: (r, 0)),
        out_shape=jax.ShapeDtypeStruct((F + pad, D), BF16),
        compiler_params=_cparams("parallel"),
        name="cast_pad_rows",
    )(w)


FF_ALIGN = 512


def kernel(x_prompt, x_sample, state_hgrn, cache_conv, a_w_in, a_ln_g, a_ln_b, a_w_s, a_b_s, a_w_out,
           b_w_in, b_lb, b_gn_g, b_w_out, f_w_up, f_conv_w, f_conv_b, f_w_down, ln1_g, ln1_b, ln2_g, ln2_b):
    depth = ln1_g.shape[0]
    alpha = (2 * depth) ** 0.25
    bp, lp, D = x_prompt.shape
    bs, ls, _ = x_sample.shape
    blk = a_w_s.shape[-1]
    assert blk == LANES and lp % blk == 0 and blk % ls == 0 and (bs * ls) % blk == 0
    d_ff = f_conv_w.shape[-1]
    f_pad = -d_ff % FF_ALIGN
    H = b_gn_g.shape[-1] // LANES

    xp = x_prompt.reshape(bp * lp, D)
    xs = x_sample.reshape(bs * ls, D)
    groups = [[xp, xp.astype(BF16), bp, lp], [xs, xs.astype(BF16), bs, ls]]

    v_rows, s_prompt, s_sample, c_prompt, c_sample = [], [], [], [], []
    for i in range(depth):
        j = i // 2
        row = lambda a: a[i][None, :]
        if i % 2 == 0:
            w_in = a_w_in[j].astype(BF16)
            w_out = a_w_out[j].astype(BF16)
            ln_g, ln_b = a_ln_g[j][None, :], a_ln_b[j][None, :]
            reps = blk // ls
            ws_variants = [a_w_s[j], jnp.tile(a_w_s[j][:, :ls, :ls], (1, reps, reps))]
            bs_variants = [a_b_s[j].T, jnp.tile(a_b_s[j][:, :ls], (1, reps)).T]
            for gi, grp in enumerate(groups):
                xf, xb, nseq, slen = grp
                u, v = _a_in(xb, w_in)
                gated, vn = _spatial(u, v, ln_g, ln_b, ws_variants[gi], bs_variants[gi],
                                     period=min(slen, blk), want_vn=gi == 1)
                if gi == 1:
                    v_rows.append(vn.reshape(nseq, slen, -1))
                grp[0], grp[1] = _mm_res_ln(gated, w_out, xf, row(ln1_g), row(ln1_b), alpha, head_major=False)
        else:
            w_in = b_w_in[j].astype(BF16)
            w_out = b_w_out[j].astype(BF16)
            gn = b_gn_g[j].reshape(H, 1, LANES)
            for gi, grp in enumerate(groups):
                xf, xb, nseq, slen = grp
                q, k, lf, v, sg = _hgrn_in(xb, w_in, b_lb, i)
                o, s_fin = _gla(q, k, lf, v, sg, gn, nseq, slen, state_hgrn[j] if gi == 1 else None)
                (s_sample if gi == 1 else s_prompt).append(s_fin)
                grp[0], grp[1] = _mm_res_ln(o, w_out, xf, row(ln1_g), row(ln1_b), alpha, head_major=True)

        pad_cols = lambda a: jnp.pad(a, ((0, 0), (0, f_pad)))
        w_up = _cast_pad_halves(f_w_up[i], f_pad)
        cw = pad_cols(f_conv_w[i])
        cb = pad_cols(f_conv_b[i][None, :])
        wd = _cast_pad_rows(f_w_down[i], f_pad)
        for gi, grp in enumerate(groups):
            xf, xb, nseq, slen = grp
            state = None
            if gi == 1:
                state = jnp.pad(cache_conv[i], ((0, 0), (SUBLANES - 2, 0), (0, f_pad)))
            gated, tail = _ffn_up(xb, w_up, cw, cb, nseq, slen, state)
            (c_sample if gi == 1 else c_prompt).append(tail[:, SUBLANES - 2:, :d_ff])
            grp[0], grp[1] = _mm_res_ln(gated, wd, xf, row(ln2_g), row(ln2_b), alpha, head_major=False)

    return (groups[0][0].reshape(bp, lp, D), groups[1][0].reshape(bs, ls, D), jnp.stack(v_rows),
            jnp.stack(s_prompt), jnp.stack(s_sample), jnp.stack(c_prompt), jnp.stack(c_sample))
```

```python
import functools

import jax
import jax.numpy as jnp
from jax import lax
from jax.experimental import pallas as pl
from jax.experimental.pallas import tpu as pltpu

F32 = jnp.float32
BF16 = jnp.bfloat16

LN_EPS = 1e-5
RMS_EPS = 1e-6
LANES = 128
SUBLANES = 8
VMEM_LIMIT_BYTES = 56 * 1024 * 1024
NEG_BIG = -1e30


def _cparams(*sem):
    return pltpu.CompilerParams(dimension_semantics=sem, vmem_limit_bytes=VMEM_LIMIT_BYTES)


def _tile(n, pref):
    t = min(n, pref)
    while n % t:
        t -= LANES
    return t


def _dot(a, b):
    return jnp.dot(a, b, preferred_element_type=F32)


def _log2(n):
    assert n > 0 and n & (n - 1) == 0, f"{n} is not a power of two"
    return n.bit_length() - 1


def _block_id(idx, size):
    return lax.shift_right_logical(idx, _log2(size))


def _layer_norm_rows(x, g, b):
    mu = jnp.mean(x, axis=-1, keepdims=True)
    xc = x - mu
    var = jnp.mean(xc * xc, axis=-1, keepdims=True)
    return xc * lax.rsqrt(var + LN_EPS) * g + b


A_SUB_COLS = 256


def _a_in_kernel(x_ref, wu_ref, wv_ref, u_ref, v_ref):
    x = x_ref[...]
    sub = _tile(u_ref.shape[1], A_SUB_COLS)
    for w_ref, o_ref in ((wu_ref, u_ref), (wv_ref, v_ref)):
        for j in range(u_ref.shape[1] // sub):
            cs = slice(j * sub, (j + 1) * sub)
            o_ref[:, cs] = jax.nn.gelu(_dot(x, w_ref[:, cs])).astype(o_ref.dtype)


def _a_in(xb, w_in):
    T, D = xb.shape
    W = w_in.shape[1] // 2
    tm, tn = _tile(T, 1024), _tile(W, 512)
    nn = W // tn
    return pl.pallas_call(
        _a_in_kernel,
        grid=(T // tm, nn),
        in_specs=[pl.BlockSpec((tm, D), lambda m, n: (m, 0)),
                  pl.BlockSpec((D, tn), lambda m, n: (0, n)),
                  pl.BlockSpec((D, tn), lambda m, n: (0, n + nn))],
        out_specs=[pl.BlockSpec((tm, tn), lambda m, n: (m, n)),
                   pl.BlockSpec((tm, tn), lambda m, n: (m, n))],
        out_shape=[jax.ShapeDtypeStruct((T, W), BF16), jax.ShapeDtypeStruct((T, W), F32)],
        compiler_params=_cparams("parallel", "arbitrary"),
        name="a_in",
    )(xb, w_in, w_in)


def _spatial_kernel(u_ref, v_ref, g_ref, b_ref, ws_ref, bs_ref, o_ref, vn_ref, *, nb, groups, period):
    vn_ref[...] = _layer_norm_rows(v_ref[...], g_ref[...], b_ref[...])
    row = lax.broadcasted_iota(jnp.int32, (LANES, LANES), 0)
    col = lax.broadcasted_iota(jnp.int32, (LANES, LANES), 1)
    keep = (col <= row) & (_block_id(row, period) == _block_id(col, period))
    for g in range(groups):
        cs = slice(g * LANES, (g + 1) * LANES)
        wsg = jnp.where(keep, ws_ref[g], 0.0).astype(BF16)
        bias = bs_ref[:, g:g + 1]
        for j in range(nb):
            rs = slice(j * LANES, (j + 1) * LANES)
            mixed = _dot(wsg, vn_ref[rs, cs].astype(BF16)) + bias
            o_ref[rs, cs] = (u_ref[rs, cs].astype(F32) * mixed).astype(o_ref.dtype)


def _spatial(u, v, ln_g, ln_b, ws, bs_t, period, want_vn):
    T, W = v.shape
    G = ws.shape[0]
    tb = _tile(T, 2 * LANES)
    row_spec = pl.BlockSpec((tb, W), lambda i: (i, 0))
    vec_spec = pl.BlockSpec((1, W), lambda i: (0, 0))
    kern = functools.partial(_spatial_kernel, nb=tb // LANES, groups=G, period=period)
    common = dict(
        grid=(T // tb,),
        in_specs=[row_spec, row_spec, vec_spec, vec_spec,
                  pl.BlockSpec((G, LANES, LANES), lambda i: (0, 0, 0)),
                  pl.BlockSpec((LANES, G), lambda i: (0, 0))],
        compiler_params=_cparams("parallel"),
    )
    if want_vn:
        return pl.pallas_call(
            kern, out_specs=[row_spec, row_spec],
            out_shape=[jax.ShapeDtypeStruct((T, W), BF16), jax.ShapeDtypeStruct((T, W), F32)],
            name="spatial_vn", **common)(u, v, ln_g, ln_b, ws, bs_t)
    gated = pl.pallas_call(
        kern, out_specs=row_spec, out_shape=jax.ShapeDtypeStruct((T, W), BF16),
        scratch_shapes=[pltpu.VMEM((tb, W), F32)],
        name="spatial", **common)(u, v, ln_g, ln_b, ws, bs_t)
    return gated, None


LN_ROWS = 64
LN_COLS = 1024
MM_K_TILE = 2816


def _ln_inplace(src_ref, g_ref, b_ref, of_ref, ob_ref, rows):
    n = src_ref.shape[1]
    cw = _tile(n, LN_COLS)
    slabs = [slice(j * cw, (j + 1) * cw) for j in range(n // cw)]

    def body(i, c):
        rs = pl.ds(pl.multiple_of(i * LN_ROWS, LN_ROWS), LN_ROWS)
        mu = sum(jnp.sum(src_ref[rs, cs], axis=-1, keepdims=True) for cs in slabs) * (1.0 / n)
        var = sum(jnp.sum(jnp.square(src_ref[rs, cs] - mu), axis=-1, keepdims=True) for cs in slabs) * (1.0 / n)
        r = lax.rsqrt(var + LN_EPS)
        for cs in slabs:
            y = (src_ref[rs, cs] - mu) * r * g_ref[:, cs] + b_ref[:, cs]
            of_ref[rs, cs] = y
            ob_ref[rs, cs] = y.astype(ob_ref.dtype)
        return c

    lax.fori_loop(0, rows // LN_ROWS, body, 0)


def _mm_res_ln_kernel(x_ref, w_ref, res_ref, g_ref, b_ref, of_ref, ob_ref, *scratch,
                      alpha, nk, nn, tn, heads):
    k = pl.program_id(1)
    n = pl.program_id(2)
    if heads:
        xs_ref, = scratch

        @pl.when(n == 0)
        def _():
            for h in range(heads):
                xs_ref[:, h * LANES:(h + 1) * LANES] = x_ref[h]

        x = xs_ref[...]
    else:
        x = x_ref[...]
    cols = pl.ds(pl.multiple_of(n * tn, tn), tn)

    @pl.when(k == 0)
    def _():
        of_ref[:, cols] = alpha * res_ref[...]

    of_ref[:, cols] += _dot(x, w_ref[...])

    @pl.when((k == nk - 1) & (n == nn - 1))
    def _():
        _ln_inplace(of_ref, g_ref, b_ref, of_ref, ob_ref, of_ref.shape[0])


def _mm_res_ln(x, w, res, ln_g, ln_b, alpha, head_major):
    T, N = res.shape
    K = w.shape[0]
    tm, tn = _tile(T, 1024), _tile(N, 512)
    tk = K if K <= 4096 else _tile(K, MM_K_TILE)
    nk, nn = K // tk, N // tn
    heads = K // LANES if head_major else 0
    once = dict(pipeline_mode=pl.Buffered(1))
    if head_major:
        assert nk == 1
        x_spec = pl.BlockSpec((heads, tm, LANES), lambda m, k, n: (0, m, 0), **once)
        scratch = [pltpu.VMEM((tm, K), BF16)]
    elif nk == 1:
        x_spec = pl.BlockSpec((tm, K), lambda m, k, n: (m, 0), **once)
        scratch = []
    else:
        x_spec = pl.BlockSpec((tm, tk), lambda m, k, n: (m, k))
        scratch = []
    row_spec = pl.BlockSpec((tm, N), lambda m, k, n: (m, 0), **once)
    vec_spec = pl.BlockSpec((1, N), lambda m, k, n: (0, 0))
    res_spec = pl.BlockSpec((tm, tn), lambda m, k, n: (m, jnp.where(k == 0, n, nn - 1)))
    return pl.pallas_call(
        functools.partial(_mm_res_ln_kernel, alpha=alpha, nk=nk, nn=nn, tn=tn, heads=heads),
        grid=(T // tm, nk, nn),
        in_specs=[x_spec,
                  pl.BlockSpec((tk, tn), lambda m, k, n: (k, n)),
                  res_spec, vec_spec, vec_spec],
        out_specs=[row_spec, row_spec],
        out_shape=[jax.ShapeDtypeStruct((T, N), F32), jax.ShapeDtypeStruct((T, N), BF16)],
        scratch_shapes=scratch,
        compiler_params=_cparams("parallel", "arbitrary", "arbitrary"),
        name="mm_res_ln",
    )(x, w, res, ln_g, ln_b)


FFN_SUB_COLS = 256


def _ffn_up_kernel(x_ref, wa_ref, wu_ref, cw_ref, cb_ref, *rest, nseq, slen, tiles_per_seq, carry):
    if carry:
        g_ref, cs_ref, halo_ref = rest
    else:
        st_ref, g_ref, cs_ref = rest
    m = pl.program_id(0)
    f = pl.program_id(1)
    tm, tf = g_ref.shape
    sub = _tile(tf, FFN_SUB_COLS)

    if carry:
        @pl.when(m % tiles_per_seq == 0)
        def _():
            halo_ref[f] = jnp.zeros((SUBLANES, tf), F32)

    _log2(slen)
    pos = lax.broadcasted_iota(jnp.int32, (tm, 1), 0) & (slen - 1)
    x = x_ref[...]
    acts = []
    for j in range(tf // sub):
        cs = slice(j * sub, (j + 1) * sub)
        a = _dot(x, wa_ref[:, cs])
        if carry:
            st = halo_ref[f, :, cs][None]
        else:
            st = st_ref[:, :, cs]

        def expand(rows):
            return jnp.broadcast_to(rows, (nseq, slen, sub)).reshape(tm, sub)

        s_m1 = expand(st[:, SUBLANES - 1:SUBLANES, :])
        s_m2 = expand(st[:, SUBLANES - 2:SUBLANES - 1, :])
        prev1 = jnp.where(pos == 0, s_m1, pltpu.roll(a, 1, 0))
        prev2 = jnp.where(pos == 0, s_m2, jnp.where(pos == 1, s_m1, pltpu.roll(a, 2, 0)))
        cw = cw_ref[:, cs]
        c = cb_ref[:, cs] + prev2 * cw[0:1, :] + prev1 * cw[1:2, :] + a * cw[2:3, :]

        tail = a.reshape(nseq, slen, sub)[:, slen - SUBLANES:, :]
        cs_ref[:, :, cs] = tail
        if carry:
            halo_ref[f, :, cs] = tail[0]
        acts.append(jax.nn.silu(c))
    for j, act in enumerate(acts):
        cs = slice(j * sub, (j + 1) * sub)
        g_ref[:, cs] = (act * _dot(x, wu_ref[:, cs])).astype(g_ref.dtype)


def _ffn_up(xb, w_up, cw, cb, nseq, slen, state):
    T, D = xb.shape
    F = w_up.shape[1] // 2
    tf = _tile(F, 512)
    nf = F // tf
    carry = state is None
    if carry:
        tm = _tile(slen, 1024)
        tiles_per_seq = slen // tm
        seq_per_tile = 1
        cs_spec = pl.BlockSpec((1, SUBLANES, tf), lambda m, f: (m, 0, f))
        extra_in, extra_specs = [], []
        scratch = [pltpu.VMEM((nf, SUBLANES, tf), F32)]
    else:
        tm = _tile(T, 1024)
        assert tm % slen == 0
        tiles_per_seq = 1
        seq_per_tile = tm // slen
        cs_spec = pl.BlockSpec((seq_per_tile, SUBLANES, tf), lambda m, f: (m, 0, f))
        extra_in, extra_specs = [state], [cs_spec]
        scratch = []
    kern = functools.partial(_ffn_up_kernel, nseq=seq_per_tile, slen=tm // seq_per_tile,
                             tiles_per_seq=tiles_per_seq, carry=carry)
    n_tails = (T // tm) * seq_per_tile
    gated, tails = pl.pallas_call(
        kern,
        grid=(T // tm, nf),
        in_specs=[pl.BlockSpec((tm, D), lambda m, f: (m, 0)),
                  pl.BlockSpec((D, tf), lambda m, f: (0, f)),
                  pl.BlockSpec((D, tf), lambda m, f: (0, f + nf)),
                  pl.BlockSpec((3, tf), lambda m, f: (0, f)),
                  pl.BlockSpec((1, tf), lambda m, f: (0, f))] + extra_specs,
        out_specs=[pl.BlockSpec((tm, tf), lambda m, f: (m, f)), cs_spec],
        out_shape=[jax.ShapeDtypeStruct((T, F), BF16), jax.ShapeDtypeStruct((n_tails, SUBLANES, F), F32)],
        scratch_shapes=scratch,
        compiler_params=_cparams("arbitrary", "arbitrary"),
        name="ffn_up_carry" if carry else "ffn_up_state",
    )(xb, w_up, w_up, cw, cb, *extra_in)
    return gated, tails[tiles_per_seq - 1::tiles_per_seq]


def _hgrn_in_kernel(x_ref, wq_ref, wf_ref, wi_ref, wg_ref, lb_ref,
                    q_ref, k_ref, lf_ref, v_ref, sg_ref, *, layer, hh):
    x = x_ref[...]
    lbv = lb_ref[...]
    e = jnp.exp(lbv - jnp.max(lbv, axis=0, keepdims=True))
    p = e / jnp.sum(e, axis=0, keepdims=True)
    lb = jnp.sum(p[1:layer + 1, :], axis=0, keepdims=True)

    def put(ref, val):
        for j in range(hh):
            ref[j] = val[:, j * LANES:(j + 1) * LANES].astype(ref.dtype)

    put(q_ref, _dot(x, wq_ref[...]))
    f = lb + (1.0 - lb) * jax.nn.sigmoid(_dot(x, wf_ref[...]))
    put(lf_ref, jnp.log(f))
    put(k_ref, 1.0 - f)
    put(v_ref, _dot(x, wi_ref[...]))
    put(sg_ref, jax.nn.silu(_dot(x, wg_ref[...])))


def _hgrn_in(xb, w_in, b_lb, layer):
    T, D = xb.shape
    QK = w_in.shape[1] // 4
    H = QK // LANES
    tm, tn = _tile(T, 1024), _tile(QK, 256)
    nn = QK // tn
    hh = tn // LANES
    depth = b_lb.shape[0]
    w_specs = [pl.BlockSpec((D, tn), lambda m, n, k=k: (0, n + k * nn)) for k in range(4)]
    o_spec = pl.BlockSpec((hh, tm, LANES), lambda m, n: (n, m, 0))
    hm = lambda dt: jax.ShapeDtypeStruct((H, T, LANES), dt)
    return pl.pallas_call(
        functools.partial(_hgrn_in_kernel, layer=layer, hh=hh),
        grid=(T // tm, nn),
        in_specs=[pl.BlockSpec((tm, D), lambda m, n: (m, 0))] + w_specs
                 + [pl.BlockSpec((depth, tn), lambda m, n: (0, n))],
        out_specs=[o_spec] * 5,
        out_shape=[hm(BF16), hm(BF16), hm(F32), hm(BF16), hm(BF16)],
        compiler_params=_cparams("parallel", "arbitrary"),
        name="hgrn_in",
    )(xb, w_in, w_in, w_in, w_in, b_lb)


GLA_BLOCK = 32
GLA_MAX_SPREAD = 60.0
GLA_HEADS_PER_ITER = 4

def _nt_dot(a, b):
    return lax.dot_general(a, b, (((1,), (1,)), ((), ())), preferred_element_type=F32)


def _tn_dot(a, b):
    return lax.dot_general(a, b, (((0,), (0,)), ((), ())), preferred_element_type=F32)


def _gla_kernel(q_ref, k_ref, lf_ref, v_ref, sg_ref, gn_ref, *rest, C, hh, has_init):
    if has_init:
        s0_ref, o_ref, sout_ref, st_ref, acc_ref = rest
    else:
        o_ref, sout_ref, st_ref, acc_ref = rest
    t = pl.program_id(2)

    @pl.when(t == 0)
    def _():
        if has_init:
            def init(h, c):
                st_ref[h] = s0_ref[0, h].T
                return c
            lax.fori_loop(0, hh, init, 0)
        else:
            st_ref[...] = jnp.zeros(st_ref.shape, F32)

    row = lax.broadcasted_iota(jnp.int32, (C, C), 0)
    col = lax.broadcasted_iota(jnp.int32, (C, C), 1)
    tri = jnp.where(col <= row, 1.0, 0.0).astype(BF16)
    n8 = C // SUBLANES
    sub = lax.broadcasted_iota(jnp.int32, (1, SUBLANES, 1), 1)
    half = C // 2
    hrow = lax.broadcasted_iota(jnp.int32, (half, half), 0)
    hcol = lax.broadcasted_iota(jnp.int32, (half, half), 1)
    w0 = min(GLA_BLOCK, C)
    nb0 = C // w0
    dr = min(C, LANES)
    drow = lax.broadcasted_iota(jnp.int32, (dr, dr), 0)
    dcol = lax.broadcasted_iota(jnp.int32, (dr, dr), 1)

    def widths(lo, hi):
        out = []
        while lo < hi:
            out.append(lo)
            lo *= 2
        return out

    def halves(x, w, which):
        return jnp.concatenate([x[(2 * p + which) * w:(2 * p + which + 1) * w] for p in range(C // (2 * w))], axis=0)

    def rows_at(x, first, step, reps):
        return jnp.concatenate(
            [jnp.broadcast_to(x[i:i + 1, :], (reps, LANES)) for i in range(first, C, step)], axis=0)

    def load(h):
        q = q_ref[h].astype(F32)
        k = k_ref[h].astype(F32)
        vb = v_ref[h]
        g = lf_ref[h]
        g1 = g.astype(BF16)
        r1 = g - g1.astype(F32)
        g2 = r1.astype(BF16)
        g3 = (r1 - g2.astype(F32)).astype(BF16)
        b = _dot(tri, g1) + _dot(tri, g2) + _dot(tri, g3)
        return q, k, vb, vb.astype(F32), b

    def level(w, q, k, v32, b):
        bref = rows_at(b, w - 1, 2 * w, w)
        qt = (halves(q, w, 1) * jnp.exp(halves(b, w, 1) - bref)).astype(BF16)
        ks = (halves(k, w, 0) * jnp.exp(bref - halves(b, w, 0))).astype(BF16)
        att = _nt_dot(qt, ks)
        if C > 2 * w:
            att = jnp.where(_block_id(hrow, w) == _block_id(hcol, w), att, 0.0)
        return _dot(att.astype(BF16), halves(v32, w, 0).astype(BF16))

    def finish(h, o):
        o = o * lax.rsqrt(jnp.mean(o * o, axis=-1, keepdims=True) + RMS_EPS)
        o_ref[h] = (o * gn_ref[h] * sg_ref[h].astype(F32)).astype(o_ref.dtype)

    def common(h, slot):
        q, k, vb, v32, b = load(h)
        b_last = b[C - 1:C, :]
        st = st_ref[h]
        o_state = _nt_dot((q * jnp.exp(b)).astype(BF16), st.astype(BF16))
        acc_ref[slot] = o_state
        kd = (k * jnp.exp(b_last - b)).astype(BF16)
        st_new = st * jnp.exp(b_last) + _tn_dot(vb, kd)
        st_ref[h] = st_new
        sout_ref[0, h] = st_new.T

        blocks = [o_state[i * w0:(i + 1) * w0] for i in range(nb0)]
        for w in widths(w0, C):
            ot = level(w, q, k, v32, b)
            for p in range(C // (2 * w)):
                for i in range(w // w0):
                    blocks[(2 * p + 1) * w // w0 + i] += ot[p * w + i * w0:p * w + (i + 1) * w0]
        bref = rows_at(b, 0, w0, w0)
        qe = (q * jnp.exp(b - bref)).astype(BF16)
        ke = (k * jnp.exp(jnp.minimum(bref - b, GLA_MAX_SPREAD))).astype(BF16)
        keep = (_block_id(drow, w0) == _block_id(dcol, w0)) & (dcol <= drow)
        for r in range(C // dr):
            rs = slice(r * dr, (r + 1) * dr)
            od = _dot(jnp.where(keep, _nt_dot(qe[rs], ke[rs]), 0.0).astype(BF16), vb[rs])
            for i in range(dr // w0):
                blocks[r * dr // w0 + i] += od[i * w0:(i + 1) * w0]
        finish(h, jnp.concatenate(blocks, axis=0))
        return jnp.max(rows_at(b, 0, w0, 1) - rows_at(b, w0 - 1, w0, 1))

    def exact(h, slot):
        q, k, _, v32, b = load(h)
        for w in widths(SUBLANES, C):
            ot = level(w, q, k, v32, b)
            for p in range(C // (2 * w)):
                acc_ref[slot, (2 * p + 1) * w:(2 * p + 2) * w, :] += ot[p * w:(p + 1) * w, :]
        b3 = b.reshape(n8, SUBLANES, LANES)
        q3 = q.reshape(n8, SUBLANES, LANES)
        k3 = k.reshape(n8, SUBLANES, LANES)
        v3 = v32.reshape(n8, SUBLANES, LANES)
        o3 = jnp.zeros((n8, SUBLANES, LANES), F32)
        for s in range(SUBLANES):
            d = jnp.where(sub >= s, b3 - b3[:, s:s + 1, :], NEG_BIG)
            e = q3 * jnp.exp(d) * k3[:, s:s + 1, :]
            o3 = o3 + jnp.sum(e, axis=-1, keepdims=True) * v3[:, s:s + 1, :]
        finish(h, acc_ref[slot] + o3.reshape(C, LANES))

    def heads(i, c):
        hs = [i * GLA_HEADS_PER_ITER + j for j in range(GLA_HEADS_PER_ITER)]
        spreads = [common(h, slot) for slot, h in enumerate(hs)]
        for slot, (h, spread) in enumerate(zip(hs, spreads)):
            @pl.when(jnp.logical_not(spread <= GLA_MAX_SPREAD))
            def _(h=h, slot=slot):
                exact(h, slot)
        return c

    lax.fori_loop(0, hh // GLA_HEADS_PER_ITER, heads, 0)


def _gla(q, k, lf, v, sg, gn, nseq, slen, s0):
    H, T, _ = q.shape
    C = _tile(slen, 256)
    tiles = slen // C
    hh = _tile(H * LANES, 16 * LANES) // LANES
    row_spec = pl.BlockSpec((hh, C, LANES), lambda s, g, t: (g, s * tiles + t, 0))
    st_spec = pl.BlockSpec((1, hh, LANES, LANES), lambda s, g, t: (s, g, 0, 0))
    in_specs = [row_spec] * 5 + [pl.BlockSpec((hh, 1, LANES), lambda s, g, t: (g, 0, 0))]
    args = [q, k, lf, v, sg, gn]
    if s0 is not None:
        in_specs.append(st_spec)
        args.append(s0)
    return pl.pallas_call(
        functools.partial(_gla_kernel, C=C, hh=hh, has_init=s0 is not None),
        grid=(nseq, H // hh, tiles),
        in_specs=in_specs,
        out_specs=[row_spec, st_spec],
        out_shape=[jax.ShapeDtypeStruct((H, T, LANES), BF16),
                   jax.ShapeDtypeStruct((nseq, H, LANES, LANES), F32)],
        scratch_shapes=[pltpu.VMEM((hh, LANES, LANES), F32), pltpu.VMEM((GLA_HEADS_PER_ITER, C, LANES), F32)],
        compiler_params=_cparams("parallel", "parallel", "arbitrary"),
        name="gla_init" if s0 is not None else "gla",
    )(*args)


PREP_ROWS = 64
PREP_BLOCK = 256


def _cast_pad_halves_kernel(w_ref, o_ref):
    n = w_ref.shape[1]
    o_ref[:, :n] = w_ref[...].astype(o_ref.dtype)
    if o_ref.shape[1] > n:
        o_ref[:, n:] = jnp.zeros((o_ref.shape[0], o_ref.shape[1] - n), o_ref.dtype)


def _cast_pad_halves(w, layer, pad):
    _, D, F2 = w.shape
    F = F2 // 2
    tr = _tile(D, PREP_ROWS)
    return pl.pallas_call(
        _cast_pad_halves_kernel,
        grid=(D // tr, 2),
        in_specs=[pl.BlockSpec((None, tr, F), lambda r, h: (layer, r, h))],
        out_specs=pl.BlockSpec((tr, F + pad), lambda r, h: (r, h)),
        out_shape=jax.ShapeDtypeStruct((D, 2 * (F + pad)), BF16),
        compiler_params=_cparams("parallel", "parallel"),
        name="cast_pad_halves",
    )(w)


def _cast_pad_rows_kernel(w_ref, o_ref, *, nb):
    r = pl.program_id(0)
    o_ref[...] = jnp.where(r < nb, w_ref[...], 0.0).astype(o_ref.dtype)


def _cast_pad_rows(w, layer, pad):
    _, F, D = w.shape
    rb = PREP_BLOCK
    assert F % rb == 0 and pad % rb == 0
    nb = F // rb
    return pl.pallas_call(
        functools.partial(_cast_pad_rows_kernel, nb=nb),
        grid=((F + pad) // rb,),
        in_specs=[pl.BlockSpec((None, rb, D), lambda r: (layer, jnp.minimum(r, nb - 1), 0))],
        out_specs=pl.BlockSpec((rb, D), lambda r: (r, 0)),
        out_shape=jax.ShapeDtypeStruct((F + pad, D), BF16),
        compiler_params=_cparams("parallel"),
        name="cast_pad_rows",
    )(w)


FF_ALIGN = 512


def kernel(x_prompt, x_sample, state_hgrn, cache_conv, a_w_in, a_ln_g, a_ln_b, a_w_s, a_b_s, a_w_out,
           b_w_in, b_lb, b_gn_g, b_w_out, f_w_up, f_conv_w, f_conv_b, f_w_down, ln1_g, ln1_b, ln2_g, ln2_b):
    depth = ln1_g.shape[0]
    alpha = (2 * depth) ** 0.25
    bp, lp, D = x_prompt.shape
    bs, ls, _ = x_sample.shape
    blk = a_w_s.shape[-1]
    assert blk == LANES and lp % blk == 0 and blk % ls == 0 and (bs * ls) % blk == 0
    d_ff = f_conv_w.shape[-1]
    f_pad = -d_ff % FF_ALIGN
    H = b_gn_g.shape[-1] // LANES

    xp = x_prompt.reshape(bp * lp, D)
    xs = x_sample.reshape(bs * ls, D)
    groups = [[xp, xp.astype(BF16), bp, lp], [xs, xs.astype(BF16), bs, ls]]

    v_rows, s_prompt, s_sample, c_prompt, c_sample = [], [], [], [], []
    for i in range(depth):
        j = i // 2
        row = lambda a: a[i][None, :]
        if i % 2 == 0:
            w_in = a_w_in[j].astype(BF16)
            w_out = a_w_out[j].astype(BF16)
            ln_g, ln_b = a_ln_g[j][None, :], a_ln_b[j][None, :]
            reps = blk // ls
            ws_variants = [a_w_s[j], jnp.tile(a_w_s[j][:, :ls, :ls], (1, reps, reps))]
            bs_variants = [a_b_s[j].T, jnp.tile(a_b_s[j][:, :ls], (1, reps)).T]
            for gi, grp in enumerate(groups):
                xf, xb, nseq, slen = grp
                u, v = _a_in(xb, w_in)
                gated, vn = _spatial(u, v, ln_g, ln_b, ws_variants[gi], bs_variants[gi],
                                     period=min(slen, blk), want_vn=gi == 1)
                if gi == 1:
                    v_rows.append(vn.reshape(nseq, slen, -1))
                grp[0], grp[1] = _mm_res_ln(gated, w_out, xf, row(ln1_g), row(ln1_b), alpha, head_major=False)
        else:
            w_in = b_w_in[j].astype(BF16)
            w_out = b_w_out[j].astype(BF16)
            gn = b_gn_g[j].reshape(H, 1, LANES)
            for gi, grp in enumerate(groups):
                xf, xb, nseq, slen = grp
                q, k, lf, v, sg = _hgrn_in(xb, w_in, b_lb, i)
                o, s_fin = _gla(q, k, lf, v, sg, gn, nseq, slen, state_hgrn[j] if gi == 1 else None)
                (s_sample if gi == 1 else s_prompt).append(s_fin)
                grp[0], grp[1] = _mm_res_ln(o, w_out, xf, row(ln1_g), row(ln1_b), alpha, head_major=True)

        pad_cols = lambda a: jnp.pad(a, ((0, 0), (0, f_pad)))
        w_up = _cast_pad_halves(f_w_up, i, f_pad)
        cw = pad_cols(f_conv_w[i])
        cb = pad_cols(f_conv_b[i][None, :])
        wd = _cast_pad_rows(f_w_down, i, f_pad)
        for gi, grp in enumerate(groups):
            xf, xb, nseq, slen = grp
            state = None
            if gi == 1:
                state = jnp.pad(cache_conv[i], ((0, 0), (SUBLANES - 2, 0), (0, f_pad)))
            gated, tail = _ffn_up(xb, w_up, cw, cb, nseq, slen, state)
            (c_sample if gi == 1 else c_prompt).append(tail[:, SUBLANES - 2:, :d_ff])
            grp[0], grp[1] = _mm_res_ln(gated, wd, xf, row(ln2_g), row(ln2_b), alpha, head_major=False)

    return (groups[0][0].reshape(bp, lp, D), groups[1][0].reshape(bs, ls, D), jnp.stack(v_rows),
            jnp.stack(s_prompt), jnp.stack(s_sample), jnp.stack(c_prompt), jnp.stack(c_sample))
```

```python
import functools

import jax
import jax.numpy as jnp
from jax import lax
from jax.experimental import pallas as pl
from jax.experimental.pallas import tpu as pltpu

F32 = jnp.float32
BF16 = jnp.bfloat16

LN_EPS = 1e-5
RMS_EPS = 1e-6
LANES = 128
SUBLANES = 8
VMEM_LIMIT_BYTES = 56 * 1024 * 1024
NEG_BIG = -1e30


def _cparams(*sem):
    return pltpu.CompilerParams(dimension_semantics=sem, vmem_limit_bytes=VMEM_LIMIT_BYTES)


def _tile(n, pref):
    t = min(n, pref)
    while n % t:
        t -= LANES
    return t


def _dot(a, b):
    return jnp.dot(a, b, preferred_element_type=F32)


def _log2(n):
    assert n > 0 and n & (n - 1) == 0, f"{n} is not a power of two"
    return n.bit_length() - 1


def _block_id(idx, size):
    return lax.shift_right_logical(idx, _log2(size))


def _layer_norm_rows(x, g, b):
    mu = jnp.mean(x, axis=-1, keepdims=True)
    xc = x - mu
    var = jnp.mean(xc * xc, axis=-1, keepdims=True)
    return xc * lax.rsqrt(var + LN_EPS) * g + b


A_SUB_COLS = 256


def _a_in_kernel(x_ref, wu_ref, wv_ref, u_ref, v_ref):
    x = x_ref[...]
    sub = _tile(u_ref.shape[1], A_SUB_COLS)
    for w_ref, o_ref in ((wu_ref, u_ref), (wv_ref, v_ref)):
        for j in range(u_ref.shape[1] // sub):
            cs = slice(j * sub, (j + 1) * sub)
            o_ref[:, cs] = jax.nn.gelu(_dot(x, w_ref[:, cs])).astype(o_ref.dtype)


def _a_in(xb, w_in):
    T, D = xb.shape
    W = w_in.shape[1] // 2
    tm, tn = _tile(T, 1024), _tile(W, 512)
    nn = W // tn
    return pl.pallas_call(
        _a_in_kernel,
        grid=(T // tm, nn),
        in_specs=[pl.BlockSpec((tm, D), lambda m, n: (m, 0)),
                  pl.BlockSpec((D, tn), lambda m, n: (0, n)),
                  pl.BlockSpec((D, tn), lambda m, n: (0, n + nn))],
        out_specs=[pl.BlockSpec((tm, tn), lambda m, n: (m, n)),
                   pl.BlockSpec((tm, tn), lambda m, n: (m, n))],
        out_shape=[jax.ShapeDtypeStruct((T, W), BF16), jax.ShapeDtypeStruct((T, W), F32)],
        compiler_params=_cparams("parallel", "arbitrary"),
        name="a_in",
    )(xb, w_in, w_in)


def _spatial_kernel(u_ref, v_ref, g_ref, b_ref, ws_ref, bs_ref, o_ref, vn_ref, *, nb, groups, period):
    vn_ref[...] = _layer_norm_rows(v_ref[...], g_ref[...], b_ref[...])
    row = lax.broadcasted_iota(jnp.int32, (LANES, LANES), 0)
    col = lax.broadcasted_iota(jnp.int32, (LANES, LANES), 1)
    keep = (col <= row) & (_block_id(row, period) == _block_id(col, period))
    for g in range(groups):
        cs = slice(g * LANES, (g + 1) * LANES)
        wsg = jnp.where(keep, ws_ref[g], 0.0).astype(BF16)
        bias = bs_ref[:, g:g + 1]
        for j in range(nb):
            rs = slice(j * LANES, (j + 1) * LANES)
            mixed = _dot(wsg, vn_ref[rs, cs].astype(BF16)) + bias
            o_ref[rs, cs] = (u_ref[rs, cs].astype(F32) * mixed).astype(o_ref.dtype)


def _spatial(u, v, ln_g, ln_b, ws, bs_t, period, want_vn):
    T, W = v.shape
    G = ws.shape[0]
    tb = _tile(T, 2 * LANES)
    row_spec = pl.BlockSpec((tb, W), lambda i: (i, 0))
    vec_spec = pl.BlockSpec((1, W), lambda i: (0, 0))
    kern = functools.partial(_spatial_kernel, nb=tb // LANES, groups=G, period=period)
    common = dict(
        grid=(T // tb,),
        in_specs=[row_spec, row_spec, vec_spec, vec_spec,
                  pl.BlockSpec((G, LANES, LANES), lambda i: (0, 0, 0)),
                  pl.BlockSpec((LANES, G), lambda i: (0, 0))],
        compiler_params=_cparams("parallel"),
    )
    if want_vn:
        return pl.pallas_call(
            kern, out_specs=[row_spec, row_spec],
            out_shape=[jax.ShapeDtypeStruct((T, W), BF16), jax.ShapeDtypeStruct((T, W), F32)],
            name="spatial_vn", **common)(u, v, ln_g, ln_b, ws, bs_t)
    gated = pl.pallas_call(
        kern, out_specs=row_spec, out_shape=jax.ShapeDtypeStruct((T, W), BF16),
        scratch_shapes=[pltpu.VMEM((tb, W), F32)],
        name="spatial", **common)(u, v, ln_g, ln_b, ws, bs_t)
    return gated, None


LN_ROWS = 64
LN_COLS = 1024
MM_K_TILE = 2816


def _ln_inplace(src_ref, g_ref, b_ref, of_ref, ob_ref, rows):
    n = src_ref.shape[1]
    cw = _tile(n, LN_COLS)
    slabs = [slice(j * cw, (j + 1) * cw) for j in range(n // cw)]

    def body(i, c):
        rs = pl.ds(pl.multiple_of(i * LN_ROWS, LN_ROWS), LN_ROWS)
        mu = sum(jnp.sum(src_ref[rs, cs], axis=-1, keepdims=True) for cs in slabs) * (1.0 / n)
        var = sum(jnp.sum(jnp.square(src_ref[rs, cs] - mu), axis=-1, keepdims=True) for cs in slabs) * (1.0 / n)
        r = lax.rsqrt(var + LN_EPS)
        for cs in slabs:
            y = (src_ref[rs, cs] - mu) * r * g_ref[:, cs] + b_ref[:, cs]
            of_ref[rs, cs] = y
            ob_ref[rs, cs] = y.astype(ob_ref.dtype)
        return c

    lax.fori_loop(0, rows // LN_ROWS, body, 0)


def _mm_res_ln_kernel(x_ref, w_ref, res_ref, g_ref, b_ref, of_ref, ob_ref, *scratch,
                      alpha, nk, nn, tn, heads):
    k = pl.program_id(1)
    n = pl.program_id(2)
    if heads:
        xs_ref, = scratch

        @pl.when(n == 0)
        def _():
            for h in range(heads):
                xs_ref[:, h * LANES:(h + 1) * LANES] = x_ref[h]

        x = xs_ref[...]
    else:
        x = x_ref[...]
    cols = pl.ds(pl.multiple_of(n * tn, tn), tn)

    @pl.when(k == 0)
    def _():
        of_ref[:, cols] = alpha * res_ref[...]

    of_ref[:, cols] += _dot(x, w_ref[...])

    @pl.when((k == nk - 1) & (n == nn - 1))
    def _():
        _ln_inplace(of_ref, g_ref, b_ref, of_ref, ob_ref, of_ref.shape[0])


def _mm_res_ln(x, w, res, ln_g, ln_b, alpha, head_major):
    T, N = res.shape
    K = w.shape[0]
    tm, tn = _tile(T, 1024), _tile(N, 512)
    tk = K if K <= 4096 else _tile(K, MM_K_TILE)
    nk, nn = K // tk, N // tn
    heads = K // LANES if head_major else 0
    once = dict(pipeline_mode=pl.Buffered(1))
    if head_major:
        assert nk == 1
        x_spec = pl.BlockSpec((heads, tm, LANES), lambda m, k, n: (0, m, 0), **once)
        scratch = [pltpu.VMEM((tm, K), BF16)]
    elif nk == 1:
        x_spec = pl.BlockSpec((tm, K), lambda m, k, n: (m, 0), **once)
        scratch = []
    else:
        x_spec = pl.BlockSpec((tm, tk), lambda m, k, n: (m, k))
        scratch = []
    row_spec = pl.BlockSpec((tm, N), lambda m, k, n: (m, 0), **once)
    vec_spec = pl.BlockSpec((1, N), lambda m, k, n: (0, 0))
    res_spec = pl.BlockSpec((tm, tn), lambda m, k, n: (m, jnp.where(k == 0, n, nn - 1)))
    return pl.pallas_call(
        functools.partial(_mm_res_ln_kernel, alpha=alpha, nk=nk, nn=nn, tn=tn, heads=heads),
        grid=(T // tm, nk, nn),
        in_specs=[x_spec,
                  pl.BlockSpec((tk, tn), lambda m, k, n: (k, n)),
                  res_spec, vec_spec, vec_spec],
        out_specs=[row_spec, row_spec],
        out_shape=[jax.ShapeDtypeStruct((T, N), F32), jax.ShapeDtypeStruct((T, N), BF16)],
        scratch_shapes=scratch,
        compiler_params=_cparams("parallel", "arbitrary", "arbitrary"),
        name="mm_res_ln",
    )(x, w, res, ln_g, ln_b)


FFN_SUB_COLS = 256


def _ffn_up_kernel(x_ref, wa_ref, wu_ref, cw_ref, cb_ref, *rest, nseq, slen, tiles_per_seq, carry):
    if carry:
        g_ref, cs_ref, halo_ref = rest
    else:
        st_ref, g_ref, cs_ref = rest
    m = pl.program_id(0)
    f = pl.program_id(1)
    tm, tf = g_ref.shape
    sub = _tile(tf, FFN_SUB_COLS)

    if carry:
        @pl.when(m % tiles_per_seq == 0)
        def _():
            halo_ref[f] = jnp.zeros((SUBLANES, tf), F32)

    _log2(slen)
    pos = lax.broadcasted_iota(jnp.int32, (tm, 1), 0) & (slen - 1)
    x = x_ref[...]
    acts = []
    for j in range(tf // sub):
        cs = slice(j * sub, (j + 1) * sub)
        a = _dot(x, wa_ref[:, cs])
        if carry:
            st = halo_ref[f, :, cs][None]
        else:
            st = st_ref[:, :, cs]

        def expand(rows):
            return jnp.broadcast_to(rows, (nseq, slen, sub)).reshape(tm, sub)

        s_m1 = expand(st[:, SUBLANES - 1:SUBLANES, :])
        s_m2 = expand(st[:, SUBLANES - 2:SUBLANES - 1, :])
        prev1 = jnp.where(pos == 0, s_m1, pltpu.roll(a, 1, 0))
        prev2 = jnp.where(pos == 0, s_m2, jnp.where(pos == 1, s_m1, pltpu.roll(a, 2, 0)))
        cw = cw_ref[:, cs]
        c = cb_ref[:, cs] + prev2 * cw[0:1, :] + prev1 * cw[1:2, :] + a * cw[2:3, :]

        tail = a.reshape(nseq, slen, sub)[:, slen - SUBLANES:, :]
        cs_ref[:, :, cs] = tail
        if carry:
            halo_ref[f, :, cs] = tail[0]
        acts.append(jax.nn.silu(c))
    for j, act in enumerate(acts):
        cs = slice(j * sub, (j + 1) * sub)
        g_ref[:, cs] = (act * _dot(x, wu_ref[:, cs])).astype(g_ref.dtype)


def _ffn_up(xb, w_up, cw, cb, nseq, slen, state):
    T, D = xb.shape
    F = w_up.shape[1] // 2
    tf = _tile(F, 512)
    nf = F // tf
    carry = state is None
    if carry:
        tm = _tile(slen, 1024)
        tiles_per_seq = slen // tm
        seq_per_tile = 1
        cs_spec = pl.BlockSpec((1, SUBLANES, tf), lambda m, f: (m, 0, f))
        extra_in, extra_specs = [], []
        scratch = [pltpu.VMEM((nf, SUBLANES, tf), F32)]
    else:
        tm = _tile(T, 1024)
        assert tm % slen == 0
        tiles_per_seq = 1
        seq_per_tile = tm // slen
        cs_spec = pl.BlockSpec((seq_per_tile, SUBLANES, tf), lambda m, f: (m, 0, f))
        extra_in, extra_specs = [state], [cs_spec]
        scratch = []
    kern = functools.partial(_ffn_up_kernel, nseq=seq_per_tile, slen=tm // seq_per_tile,
                             tiles_per_seq=tiles_per_seq, carry=carry)
    n_tails = (T // tm) * seq_per_tile
    gated, tails = pl.pallas_call(
        kern,
        grid=(T // tm, nf),
        in_specs=[pl.BlockSpec((tm, D), lambda m, f: (m, 0)),
                  pl.BlockSpec((D, tf), lambda m, f: (0, f)),
                  pl.BlockSpec((D, tf), lambda m, f: (0, f + nf)),
                  pl.BlockSpec((3, tf), lambda m, f: (0, f)),
                  pl.BlockSpec((1, tf), lambda m, f: (0, f))] + extra_specs,
        out_specs=[pl.BlockSpec((tm, tf), lambda m, f: (m, f)), cs_spec],
        out_shape=[jax.ShapeDtypeStruct((T, F), BF16), jax.ShapeDtypeStruct((n_tails, SUBLANES, F), F32)],
        scratch_shapes=scratch,
        compiler_params=_cparams("arbitrary", "arbitrary"),
        name="ffn_up_carry" if carry else "ffn_up_state",
    )(xb, w_up, w_up, cw, cb, *extra_in)
    return gated, tails[tiles_per_seq - 1::tiles_per_seq]


def _hgrn_in_kernel(x_ref, wq_ref, wf_ref, wi_ref, wg_ref, lb_ref,
                    q_ref, k_ref, lf_ref, v_ref, sg_ref, *, layer, hh):
    x = x_ref[...]
    lbv = lb_ref[...]
    e = jnp.exp(lbv - jnp.max(lbv, axis=0, keepdims=True))
    p = e / jnp.sum(e, axis=0, keepdims=True)
    lb = jnp.sum(p[1:layer + 1, :], axis=0, keepdims=True)

    def put(ref, val):
        for j in range(hh):
            ref[j] = val[:, j * LANES:(j + 1) * LANES].astype(ref.dtype)

    put(q_ref, _dot(x, wq_ref[...]))
    f = lb + (1.0 - lb) * jax.nn.sigmoid(_dot(x, wf_ref[...]))
    put(lf_ref, jnp.log(f))
    put(k_ref, 1.0 - f)
    put(v_ref, _dot(x, wi_ref[...]))
    put(sg_ref, jax.nn.silu(_dot(x, wg_ref[...])))


def _hgrn_in(xb, w_in, b_lb, layer):
    T, D = xb.shape
    QK = w_in.shape[1] // 4
    H = QK // LANES
    tm, tn = _tile(T, 1024), _tile(QK, 256)
    nn = QK // tn
    hh = tn // LANES
    depth = b_lb.shape[0]
    w_specs = [pl.BlockSpec((D, tn), lambda m, n, k=k: (0, n + k * nn)) for k in range(4)]
    o_spec = pl.BlockSpec((hh, tm, LANES), lambda m, n: (n, m, 0))
    hm = lambda dt: jax.ShapeDtypeStruct((H, T, LANES), dt)
    return pl.pallas_call(
        functools.partial(_hgrn_in_kernel, layer=layer, hh=hh),
        grid=(T // tm, nn),
        in_specs=[pl.BlockSpec((tm, D), lambda m, n: (m, 0))] + w_specs
                 + [pl.BlockSpec((depth, tn), lambda m, n: (0, n))],
        out_specs=[o_spec] * 5,
        out_shape=[hm(BF16), hm(BF16), hm(F32), hm(BF16), hm(BF16)],
        compiler_params=_cparams("parallel", "arbitrary"),
        name="hgrn_in",
    )(xb, w_in, w_in, w_in, w_in, b_lb)


GLA_BLOCK = 32
GLA_MAX_SPREAD = 60.0
GLA_HEADS_PER_ITER = 8

def _nt_dot(a, b):
    return lax.dot_general(a, b, (((1,), (1,)), ((), ())), preferred_element_type=F32)


def _tn_dot(a, b):
    return lax.dot_general(a, b, (((0,), (0,)), ((), ())), preferred_element_type=F32)


def _gla_kernel(q_ref, k_ref, lf_ref, v_ref, sg_ref, gn_ref, *rest, C, hh, has_init):
    if has_init:
        s0_ref, o_ref, sout_ref, st_ref, acc_ref, bsc_ref, ops_ref = rest
    else:
        o_ref, sout_ref, st_ref, acc_ref, bsc_ref, ops_ref = rest
    t = pl.program_id(2)

    @pl.when(t == 0)
    def _():
        if has_init:
            def init(h, c):
                st_ref[h] = s0_ref[0, h].T
                return c
            lax.fori_loop(0, hh, init, 0)
        else:
            st_ref[...] = jnp.zeros(st_ref.shape, F32)

    row = lax.broadcasted_iota(jnp.int32, (C, C), 0)
    col = lax.broadcasted_iota(jnp.int32, (C, C), 1)
    tri = jnp.where(col <= row, 1.0, 0.0).astype(BF16)
    n8 = C // SUBLANES
    sub = lax.broadcasted_iota(jnp.int32, (1, SUBLANES, 1), 1)
    half = C // 2
    hrow = lax.broadcasted_iota(jnp.int32, (half, half), 0)
    hcol = lax.broadcasted_iota(jnp.int32, (half, half), 1)
    w0 = min(GLA_BLOCK, C)
    nb0 = C // w0
    dr = min(C, LANES)
    drow = lax.broadcasted_iota(jnp.int32, (dr, dr), 0)
    dcol = lax.broadcasted_iota(jnp.int32, (dr, dr), 1)

    def widths(lo, hi):
        out = []
        while lo < hi:
            out.append(lo)
            lo *= 2
        return out

    def halves(x, w, which):
        return jnp.concatenate([x[(2 * p + which) * w:(2 * p + which + 1) * w] for p in range(C // (2 * w))], axis=0)

    def rows_at(x, first, step, reps):
        return jnp.concatenate(
            [jnp.broadcast_to(x[i:i + 1, :], (reps, LANES)) for i in range(first, C, step)], axis=0)

    def load(h):
        q = q_ref[h].astype(F32)
        k = k_ref[h].astype(F32)
        vb = v_ref[h]
        g = lf_ref[h]
        g1 = g.astype(BF16)
        r1 = g - g1.astype(F32)
        g2 = r1.astype(BF16)
        g3 = (r1 - g2.astype(F32)).astype(BF16)
        b = _dot(tri, g1) + _dot(tri, g2) + _dot(tri, g3)
        return q, k, vb, vb.astype(F32), b

    def level(w, q, k, v32, b):
        bref = rows_at(b, w - 1, 2 * w, w)
        qt = (halves(q, w, 1) * jnp.exp(halves(b, w, 1) - bref)).astype(BF16)
        ks = (halves(k, w, 0) * jnp.exp(bref - halves(b, w, 0))).astype(BF16)
        att = _nt_dot(qt, ks)
        if C > 2 * w:
            att = jnp.where(_block_id(hrow, w) == _block_id(hcol, w), att, 0.0)
        return _dot(att.astype(BF16), halves(v32, w, 0).astype(BF16))

    def finish(h, o):
        o = o * lax.rsqrt(jnp.mean(o * o, axis=-1, keepdims=True) + RMS_EPS)
        o_ref[h] = (o * gn_ref[h] * sg_ref[h].astype(F32)).astype(o_ref.dtype)

    levels = widths(w0, C)
    Q_STATE, K_STATE, Q_BLOCK, K_BLOCK = 0, 1, 2, 3
    q_level = lambda l: 4 + 2 * l
    k_level = lambda l: 5 + 2 * l

    def prepare(h, slot):
        g = lf_ref[h]
        g1 = g.astype(BF16)
        r1 = g - g1.astype(F32)
        g2 = r1.astype(BF16)
        g3 = (r1 - g2.astype(F32)).astype(BF16)
        bsc_ref[slot] = _dot(tri, g1) + _dot(tri, g2) + _dot(tri, g3)
        b_end = bsc_ref[slot, C - 1:C, :]
        spread = None
        for i in range(nb0):
            rs = slice(i * w0, (i + 1) * w0)
            bc = bsc_ref[slot, rs, :]
            bf, bl = bc[0:1, :], bc[w0 - 1:w0, :]
            qp = q_ref[h, rs, :].astype(F32) * jnp.exp(bc - bf)
            kc = k_ref[h, rs, :].astype(F32)
            kq = kc * jnp.exp(bl - bc)
            ops_ref[slot, Q_STATE, rs, :] = (qp * jnp.exp(bf)).astype(BF16)
            ops_ref[slot, K_STATE, rs, :] = (kq * jnp.exp(b_end - bl)).astype(BF16)
            ops_ref[slot, Q_BLOCK, rs, :] = qp.astype(BF16)
            ops_ref[slot, K_BLOCK, rs, :] = (kc * jnp.exp(jnp.minimum(bf - bc, GLA_MAX_SPREAD))).astype(BF16)
            for l, w in enumerate(levels):
                p, off = divmod(i * w0, 2 * w)
                ref_row = p * 2 * w + w - 1
                br = bsc_ref[slot, ref_row:ref_row + 1, :]
                if off >= w:
                    dst = slice(p * w + off - w, p * w + off - w + w0)
                    ops_ref[slot, q_level(l), dst, :] = (qp * jnp.exp(bf - br)).astype(BF16)
                else:
                    dst = slice(p * w + off, p * w + off + w0)
                    ops_ref[slot, k_level(l), dst, :] = (kq * jnp.exp(br - bl)).astype(BF16)
            spread = bf - bl if spread is None else jnp.maximum(spread, bf - bl)
        return jnp.max(spread)

    def attend(h, slot):
        vb = v_ref[h]
        st = st_ref[h]
        o_state = _nt_dot(ops_ref[slot, Q_STATE], st.astype(BF16))
        acc_ref[slot] = o_state
        st_new = st * jnp.exp(bsc_ref[slot, C - 1:C, :]) + _tn_dot(vb, ops_ref[slot, K_STATE])
        st_ref[h] = st_new
        sout_ref[0, h] = st_new.T
        blk = [o_state[i * w0:(i + 1) * w0] for i in range(nb0)]
        for l, w in enumerate(levels):
            att = _nt_dot(ops_ref[slot, q_level(l), 0:half, :], ops_ref[slot, k_level(l), 0:half, :])
            if C > 2 * w:
                att = jnp.where(_block_id(hrow, w) == _block_id(hcol, w), att, 0.0)
            ot = _dot(att.astype(BF16), halves(vb, w, 0))
            for p in range(C // (2 * w)):
                for i in range(w // w0):
                    blk[(2 * p + 1) * w // w0 + i] += ot[p * w + i * w0:p * w + (i + 1) * w0]
        keep = (_block_id(drow, w0) == _block_id(dcol, w0)) & (dcol <= drow)
        for r in range(C // dr):
            rs = slice(r * dr, (r + 1) * dr)
            att = jnp.where(keep, _nt_dot(ops_ref[slot, Q_BLOCK, rs, :], ops_ref[slot, K_BLOCK, rs, :]), 0.0)
            od = _dot(att.astype(BF16), vb[rs])
            for i in range(dr // w0):
                blk[r * dr // w0 + i] += od[i * w0:(i + 1) * w0]
        finish(h, jnp.concatenate(blk, axis=0))

    def common(hs):
        spreads = [prepare(h, slot) for slot, h in enumerate(hs)]
        for slot, h in enumerate(hs):
            attend(h, slot)
        return spreads

    def exact(h, slot):
        q, k, _, v32, b = load(h)
        for w in widths(SUBLANES, C):
            ot = level(w, q, k, v32, b)
            for p in range(C // (2 * w)):
                acc_ref[slot, (2 * p + 1) * w:(2 * p + 2) * w, :] += ot[p * w:(p + 1) * w, :]
        b3 = b.reshape(n8, SUBLANES, LANES)
        q3 = q.reshape(n8, SUBLANES, LANES)
        k3 = k.reshape(n8, SUBLANES, LANES)
        v3 = v32.reshape(n8, SUBLANES, LANES)
        o3 = jnp.zeros((n8, SUBLANES, LANES), F32)
        for s in range(SUBLANES):
            d = jnp.where(sub >= s, b3 - b3[:, s:s + 1, :], NEG_BIG)
            e = q3 * jnp.exp(d) * k3[:, s:s + 1, :]
            o3 = o3 + jnp.sum(e, axis=-1, keepdims=True) * v3[:, s:s + 1, :]
        finish(h, acc_ref[slot] + o3.reshape(C, LANES))

    def heads(i, c):
        hs = [i * GLA_HEADS_PER_ITER + j for j in range(GLA_HEADS_PER_ITER)]
        spreads = common(hs)
        for slot, (h, spread) in enumerate(zip(hs, spreads)):
            @pl.when(jnp.logical_not(spread <= GLA_MAX_SPREAD))
            def _(h=h, slot=slot):
                exact(h, slot)
        return c

    lax.fori_loop(0, hh // GLA_HEADS_PER_ITER, heads, 0)


def _gla(q, k, lf, v, sg, gn, nseq, slen, s0):
    H, T, _ = q.shape
    C = _tile(slen, 256)
    tiles = slen // C
    n_levels = _log2(C // min(GLA_BLOCK, C))
    hh = _tile(H * LANES, 16 * LANES) // LANES
    row_spec = pl.BlockSpec((hh, C, LANES), lambda s, g, t: (g, s * tiles + t, 0))
    st_spec = pl.BlockSpec((1, hh, LANES, LANES), lambda s, g, t: (s, g, 0, 0))
    in_specs = [row_spec] * 5 + [pl.BlockSpec((hh, 1, LANES), lambda s, g, t: (g, 0, 0))]
    args = [q, k, lf, v, sg, gn]
    if s0 is not None:
        in_specs.append(st_spec)
        args.append(s0)
    return pl.pallas_call(
        functools.partial(_gla_kernel, C=C, hh=hh, has_init=s0 is not None),
        grid=(nseq, H // hh, tiles),
        in_specs=in_specs,
        out_specs=[row_spec, st_spec],
        out_shape=[jax.ShapeDtypeStruct((H, T, LANES), BF16),
                   jax.ShapeDtypeStruct((nseq, H, LANES, LANES), F32)],
        scratch_shapes=[pltpu.VMEM((hh, LANES, LANES), F32),
                        pltpu.VMEM((GLA_HEADS_PER_ITER, C, LANES), F32),
                        pltpu.VMEM((GLA_HEADS_PER_ITER, C, LANES), F32),
                        pltpu.VMEM((GLA_HEADS_PER_ITER, 4 + 2 * n_levels, C, LANES), BF16)],
        compiler_params=_cparams("parallel", "parallel", "arbitrary"),
        name="gla_init" if s0 is not None else "gla",
    )(*args)


PREP_ROWS = 64
PREP_BLOCK = 256


def _cast_pad_halves_kernel(w_ref, o_ref):
    n = w_ref.shape[1]
    o_ref[:, :n] = w_ref[...].astype(o_ref.dtype)
    if o_ref.shape[1] > n:
        o_ref[:, n:] = jnp.zeros((o_ref.shape[0], o_ref.shape[1] - n), o_ref.dtype)


def _cast_pad_halves(w, layer, pad):
    _, D, F2 = w.shape
    F = F2 // 2
    tr = _tile(D, PREP_ROWS)
    return pl.pallas_call(
        _cast_pad_halves_kernel,
        grid=(D // tr, 2),
        in_specs=[pl.BlockSpec((None, tr, F), lambda r, h: (layer, r, h))],
        out_specs=pl.BlockSpec((tr, F + pad), lambda r, h: (r, h)),
        out_shape=jax.ShapeDtypeStruct((D, 2 * (F + pad)), BF16),
        compiler_params=_cparams("parallel", "parallel"),
        name="cast_pad_halves",
    )(w)


def _cast_pad_rows_kernel(w_ref, o_ref, *, nb):
    r = pl.program_id(0)
    o_ref[...] = jnp.where(r < nb, w_ref[...], 0.0).astype(o_ref.dtype)


def _cast_pad_rows(w, layer, pad):
    _, F, D = w.shape
    rb = PREP_BLOCK
    assert F % rb == 0 and pad % rb == 0
    nb = F // rb
    return pl.pallas_call(
        functools.partial(_cast_pad_rows_kernel, nb=nb),
        grid=((F + pad) // rb,),
        in_specs=[pl.BlockSpec((None, rb, D), lambda r: (layer, jnp.minimum(r, nb - 1), 0))],
        out_specs=pl.BlockSpec((rb, D), lambda r: (r, 0)),
        out_shape=jax.ShapeDtypeStruct((F + pad, D), BF16),
        compiler_params=_cparams("parallel"),
        name="cast_pad_rows",
    )(w)


FF_ALIGN = 512


def kernel(x_prompt, x_sample, state_hgrn, cache_conv, a_w_in, a_ln_g, a_ln_b, a_w_s, a_b_s, a_w_out,
           b_w_in, b_lb, b_gn_g, b_w_out, f_w_up, f_conv_w, f_conv_b, f_w_down, ln1_g, ln1_b, ln2_g, ln2_b):
    depth = ln1_g.shape[0]
    alpha = (2 * depth) ** 0.25
    bp, lp, D = x_prompt.shape
    bs, ls, _ = x_sample.shape
    blk = a_w_s.shape[-1]
    assert blk == LANES and lp % blk == 0 and blk % ls == 0 and (bs * ls) % blk == 0
    d_ff = f_conv_w.shape[-1]
    f_pad = -d_ff % FF_ALIGN
    H = b_gn_g.shape[-1] // LANES

    xp = x_prompt.reshape(bp * lp, D)
    xs = x_sample.reshape(bs * ls, D)
    groups = [[xp, xp.astype(BF16), bp, lp], [xs, xs.astype(BF16), bs, ls]]

    v_rows, s_prompt, s_sample, c_prompt, c_sample = [], [], [], [], []
    for i in range(depth):
        j = i // 2
        row = lambda a: a[i][None, :]
        if i % 2 == 0:
            w_in = a_w_in[j].astype(BF16)
            w_out = a_w_out[j].astype(BF16)
            ln_g, ln_b = a_ln_g[j][None, :], a_ln_b[j][None, :]
            reps = blk // ls
            ws_variants = [a_w_s[j], jnp.tile(a_w_s[j][:, :ls, :ls], (1, reps, reps))]
            bs_variants = [a_b_s[j].T, jnp.tile(a_b_s[j][:, :ls], (1, reps)).T]
            for gi, grp in enumerate(groups):
                xf, xb, nseq, slen = grp
                u, v = _a_in(xb, w_in)
                gated, vn = _spatial(u, v, ln_g, ln_b, ws_variants[gi], bs_variants[gi],
                                     period=min(slen, blk), want_vn=gi == 1)
                if gi == 1:
                    v_rows.append(vn.reshape(nseq, slen, -1))
                grp[0], grp[1] = _mm_res_ln(gated, w_out, xf, row(ln1_g), row(ln1_b), alpha, head_major=False)
        else:
            w_in = b_w_in[j].astype(BF16)
            w_out = b_w_out[j].astype(BF16)
            gn = b_gn_g[j].reshape(H, 1, LANES)
            for gi, grp in enumerate(groups):
                xf, xb, nseq, slen = grp
                q, k, lf, v, sg = _hgrn_in(xb, w_in, b_lb, i)
                o, s_fin = _gla(q, k, lf, v, sg, gn, nseq, slen, state_hgrn[j] if gi == 1 else None)
                (s_sample if gi == 1 else s_prompt).append(s_fin)
                grp[0], grp[1] = _mm_res_ln(o, w_out, xf, row(ln1_g), row(ln1_b), alpha, head_major=True)

        pad_cols = lambda a: jnp.pad(a, ((0, 0), (0, f_pad)))
        w_up = _cast_pad_halves(f_w_up, i, f_pad)
        cw = pad_cols(f_conv_w[i])
        cb = pad_cols(f_conv_b[i][None, :])
        wd = _cast_pad_rows(f_w_down, i, f_pad)
        for gi, grp in enumerate(groups):
            xf, xb, nseq, slen = grp
            state = None
            if gi == 1:
                state = jnp.pad(cache_conv[i], ((0, 0), (SUBLANES - 2, 0), (0, f_pad)))
            gated, tail = _ffn_up(xb, w_up, cw, cb, nseq, slen, state)
            (c_sample if gi == 1 else c_prompt).append(tail[:, SUBLANES - 2:, :d_ff])
            grp[0], grp[1] = _mm_res_ln(gated, wd, xf, row(ln2_g), row(ln2_b), alpha, head_major=False)

    return (groups[0][0].reshape(bp, lp, D), groups[1][0].reshape(bs, ls, D), jnp.stack(v_rows),
            jnp.stack(s_prompt), jnp.stack(s_sample), jnp.stack(c_prompt), jnp.stack(c_sample))
```

```python
import functools
import math

import jax
import jax.numpy as jnp
from jax import lax
from jax.experimental import pallas as pl
from jax.experimental.pallas import tpu as pltpu

F32 = jnp.float32
BF16 = jnp.bfloat16

LN_EPS = 1e-5
RMS_EPS = 1e-6
LANES = 128
SUBLANES = 8
VMEM_LIMIT_BYTES = 56 * 1024 * 1024
NEG_BIG = -1e30


def _cparams(*sem):
    return pltpu.CompilerParams(dimension_semantics=sem, vmem_limit_bytes=VMEM_LIMIT_BYTES)


def _tile(n, pref):
    t = min(n, pref)
    while n % t:
        t -= LANES
    return t


def _dot(a, b):
    return jnp.dot(a, b, preferred_element_type=F32)


def _log2(n):
    assert n > 0 and n & (n - 1) == 0, f"{n} is not a power of two"
    return n.bit_length() - 1


def _block_id(idx, size):
    return lax.shift_right_logical(idx, _log2(size))


def _layer_norm_rows(x, g, b):
    mu = jnp.mean(x, axis=-1, keepdims=True)
    xc = x - mu
    var = jnp.mean(xc * xc, axis=-1, keepdims=True)
    return xc * lax.rsqrt(var + LN_EPS) * g + b


A_SUB_COLS = 256


def _a_in_kernel(x_ref, wu_ref, wv_ref, u_ref, v_ref):
    x = x_ref[...]
    sub = _tile(u_ref.shape[1], A_SUB_COLS)
    for w_ref, o_ref in ((wu_ref, u_ref), (wv_ref, v_ref)):
        for j in range(u_ref.shape[1] // sub):
            cs = slice(j * sub, (j + 1) * sub)
            o_ref[:, cs] = jax.nn.gelu(_dot(x, w_ref[:, cs])).astype(o_ref.dtype)


def _a_in(xb, w_in):
    T, D = xb.shape
    W = w_in.shape[1] // 2
    tm, tn = _tile(T, 1024), _tile(W, 512)
    nn = W // tn
    return pl.pallas_call(
        _a_in_kernel,
        grid=(T // tm, nn),
        in_specs=[pl.BlockSpec((tm, D), lambda m, n: (m, 0)),
                  pl.BlockSpec((D, tn), lambda m, n: (0, n)),
                  pl.BlockSpec((D, tn), lambda m, n: (0, n + nn))],
        out_specs=[pl.BlockSpec((tm, tn), lambda m, n: (m, n)),
                   pl.BlockSpec((tm, tn), lambda m, n: (m, n))],
        out_shape=[jax.ShapeDtypeStruct((T, W), BF16), jax.ShapeDtypeStruct((T, W), F32)],
        compiler_params=_cparams("parallel", "arbitrary"),
        name="a_in",
    )(xb, w_in, w_in)


def _spatial_kernel(u_ref, v_ref, g_ref, b_ref, ws_ref, bs_ref, o_ref, vn_ref, *, nb, groups, period):
    vn_ref[...] = _layer_norm_rows(v_ref[...], g_ref[...], b_ref[...])
    row = lax.broadcasted_iota(jnp.int32, (LANES, LANES), 0)
    col = lax.broadcasted_iota(jnp.int32, (LANES, LANES), 1)
    keep = (col <= row) & (_block_id(row, period) == _block_id(col, period))
    for g in range(groups):
        cs = slice(g * LANES, (g + 1) * LANES)
        wsg = jnp.where(keep, ws_ref[g], 0.0).astype(BF16)
        bias = bs_ref[:, g:g + 1]
        for j in range(nb):
            rs = slice(j * LANES, (j + 1) * LANES)
            mixed = _dot(wsg, vn_ref[rs, cs].astype(BF16)) + bias
            o_ref[rs, cs] = (u_ref[rs, cs].astype(F32) * mixed).astype(o_ref.dtype)


def _spatial(u, v, ln_g, ln_b, ws, bs_t, period, want_vn):
    T, W = v.shape
    G = ws.shape[0]
    tb = _tile(T, 2 * LANES)
    row_spec = pl.BlockSpec((tb, W), lambda i: (i, 0))
    vec_spec = pl.BlockSpec((1, W), lambda i: (0, 0))
    kern = functools.partial(_spatial_kernel, nb=tb // LANES, groups=G, period=period)
    common = dict(
        grid=(T // tb,),
        in_specs=[row_spec, row_spec, vec_spec, vec_spec,
                  pl.BlockSpec((G, LANES, LANES), lambda i: (0, 0, 0)),
                  pl.BlockSpec((LANES, G), lambda i: (0, 0))],
        compiler_params=_cparams("parallel"),
    )
    if want_vn:
        return pl.pallas_call(
            kern, out_specs=[row_spec, row_spec],
            out_shape=[jax.ShapeDtypeStruct((T, W), BF16), jax.ShapeDtypeStruct((T, W), F32)],
            name="spatial_vn", **common)(u, v, ln_g, ln_b, ws, bs_t)
    gated = pl.pallas_call(
        kern, out_specs=row_spec, out_shape=jax.ShapeDtypeStruct((T, W), BF16),
        scratch_shapes=[pltpu.VMEM((tb, W), F32)],
        name="spatial", **common)(u, v, ln_g, ln_b, ws, bs_t)
    return gated, None


LN_ROWS = 64
LN_COLS = 1024
MM_K_TILE = 2816


def _ln_inplace(acc_ref, g_ref, b_ref, ob_ref, after_chunk):
    rows, n = acc_ref.shape
    cw = _tile(n, LN_COLS)
    slabs = [slice(j * cw, (j + 1) * cw) for j in range(n // cw)]

    def body(i, c):
        rs = pl.ds(pl.multiple_of(i * LN_ROWS, LN_ROWS), LN_ROWS)
        mu = sum(jnp.sum(acc_ref[rs, cs], axis=-1, keepdims=True) for cs in slabs) * (1.0 / n)
        var = sum(jnp.sum(jnp.square(acc_ref[rs, cs] - mu), axis=-1, keepdims=True) for cs in slabs) * (1.0 / n)
        r = lax.rsqrt(var + LN_EPS)
        for cs in slabs:
            y = (acc_ref[rs, cs] - mu) * r * g_ref[:, cs] + b_ref[:, cs]
            acc_ref[rs, cs] = y
            ob_ref[rs, cs] = y.astype(ob_ref.dtype)
        after_chunk(i)
        return c

    lax.fori_loop(0, rows // LN_ROWS, body, 0)


def _mm_res_ln_kernel(x_ref, w_ref, res_ref, g_ref, b_ref, of_hbm, ob_hbm, acc_ref, ob_ref, sem, *scratch,
                      alpha, nm, nk, nn, tn, heads):
    m = pl.program_id(0)
    k = pl.program_id(1)
    n = pl.program_id(2)
    tm = acc_ref.shape[0]
    n_chunks = tm // LN_ROWS

    def copies(tile, i):
        src = pl.ds(pl.multiple_of(i * LN_ROWS, LN_ROWS), LN_ROWS)
        dst = pl.ds(pl.multiple_of(tile * tm + i * LN_ROWS, LN_ROWS), LN_ROWS)
        return (pltpu.make_async_copy(acc_ref.at[src], of_hbm.at[dst], sem.at[0]),
                pltpu.make_async_copy(ob_ref.at[src], ob_hbm.at[dst], sem.at[1]))

    def start_chunk(i):
        for cp in copies(m, i):
            cp.start()

    def wait_tile(tile):
        def body(i, c):
            for cp in copies(tile, i):
                cp.wait()
            return c
        lax.fori_loop(0, n_chunks, body, 0)

    @pl.when((m > 0) & (k == 0) & (n == 0))
    def _():
        wait_tile(m - 1)

    if heads:
        xs_ref, = scratch

        @pl.when(n == 0)
        def _():
            for h in range(heads):
                xs_ref[:, h * LANES:(h + 1) * LANES] = x_ref[h]

        x = xs_ref[...]
    else:
        x = x_ref[...]
    cols = pl.ds(pl.multiple_of(n * tn, tn), tn)

    @pl.when(k == 0)
    def _():
        acc_ref[:, cols] = alpha * res_ref[...]

    acc_ref[:, cols] += _dot(x, w_ref[...])

    @pl.when((k == nk - 1) & (n == nn - 1))
    def _():
        _ln_inplace(acc_ref, g_ref, b_ref, ob_ref, start_chunk)

    @pl.when((m == nm - 1) & (k == nk - 1) & (n == nn - 1))
    def _():
        wait_tile(m)


def _mm_res_ln(x, w, res, ln_g, ln_b, alpha, head_major):
    T, N = res.shape
    K = w.shape[0]
    tm, tn = _tile(T, 1024), _tile(N, 512)
    tk = K if K <= 4096 else _tile(K, MM_K_TILE)
    nk, nn = K // tk, N // tn
    heads = K // LANES if head_major else 0
    once = dict(pipeline_mode=pl.Buffered(1))
    if head_major:
        assert nk == 1
        x_spec = pl.BlockSpec((heads, tm, LANES), lambda m, k, n: (0, m, 0), **once)
        scratch = [pltpu.VMEM((tm, K), BF16)]
    elif nk == 1:
        x_spec = pl.BlockSpec((tm, K), lambda m, k, n: (m, 0), **once)
        scratch = []
    else:
        x_spec = pl.BlockSpec((tm, tk), lambda m, k, n: (m, k))
        scratch = []
    vec_spec = pl.BlockSpec((1, N), lambda m, k, n: (0, 0))
    res_spec = pl.BlockSpec((tm, tn), lambda m, k, n: (m, jnp.where(k == 0, n, nn - 1)))
    hbm_spec = pl.BlockSpec(memory_space=pl.ANY)
    assert tm % LN_ROWS == 0
    return pl.pallas_call(
        functools.partial(_mm_res_ln_kernel, alpha=alpha, nm=T // tm, nk=nk, nn=nn, tn=tn, heads=heads),
        grid=(T // tm, nk, nn),
        in_specs=[x_spec,
                  pl.BlockSpec((tk, tn), lambda m, k, n: (k, n)),
                  res_spec, vec_spec, vec_spec],
        out_specs=[hbm_spec, hbm_spec],
        out_shape=[jax.ShapeDtypeStruct((T, N), F32), jax.ShapeDtypeStruct((T, N), BF16)],
        scratch_shapes=[pltpu.VMEM((tm, N), F32), pltpu.VMEM((tm, N), BF16),
                        pltpu.SemaphoreType.DMA((2,))] + scratch,
        compiler_params=_cparams("arbitrary", "arbitrary", "arbitrary"),
        name="mm_res_ln",
    )(x, w, res, ln_g, ln_b)


FFN_SUB_COLS = 256


def _ffn_up_kernel(x_ref, wa_ref, wu_ref, cw_ref, cb_ref, *rest, nseq, slen, tiles_per_seq, carry):
    if carry:
        g_ref, cs_ref, halo_ref = rest
    else:
        st_ref, g_ref, cs_ref = rest
    m = pl.program_id(0)
    f = pl.program_id(1)
    tm, tf = g_ref.shape
    sub = _tile(tf, FFN_SUB_COLS)

    if carry:
        @pl.when(m % tiles_per_seq == 0)
        def _():
            halo_ref[f] = jnp.zeros((SUBLANES, tf), F32)

    _log2(slen)
    pos = lax.broadcasted_iota(jnp.int32, (tm, 1), 0) & (slen - 1)
    x = x_ref[...]
    acts = []
    for j in range(tf // sub):
        cs = slice(j * sub, (j + 1) * sub)
        a = _dot(x, wa_ref[:, cs])
        if carry:
            st = halo_ref[f, :, cs][None]
        else:
            st = st_ref[:, :, cs]

        def expand(rows):
            return jnp.broadcast_to(rows, (nseq, slen, sub)).reshape(tm, sub)

        s_m1 = expand(st[:, SUBLANES - 1:SUBLANES, :])
        s_m2 = expand(st[:, SUBLANES - 2:SUBLANES - 1, :])
        prev1 = jnp.where(pos == 0, s_m1, pltpu.roll(a, 1, 0))
        prev2 = jnp.where(pos == 0, s_m2, jnp.where(pos == 1, s_m1, pltpu.roll(a, 2, 0)))
        cw = cw_ref[:, cs]
        c = cb_ref[:, cs] + prev2 * cw[0:1, :] + prev1 * cw[1:2, :] + a * cw[2:3, :]

        tail = a.reshape(nseq, slen, sub)[:, slen - SUBLANES:, :]
        cs_ref[:, :, cs] = tail
        if carry:
            halo_ref[f, :, cs] = tail[0]
        acts.append(jax.nn.silu(c))
    for j, act in enumerate(acts):
        cs = slice(j * sub, (j + 1) * sub)
        g_ref[:, cs] = (act * _dot(x, wu_ref[:, cs])).astype(g_ref.dtype)


def _ffn_up(xb, w_up, cw, cb, nseq, slen, state):
    T, D = xb.shape
    F = w_up.shape[1] // 2
    tf = _tile(F, 512)
    nf = F // tf
    carry = state is None
    if carry:
        tm = _tile(slen, 1024)
        tiles_per_seq = slen // tm
        seq_per_tile = 1
        cs_spec = pl.BlockSpec((1, SUBLANES, tf), lambda m, f: (m, 0, f))
        extra_in, extra_specs = [], []
        scratch = [pltpu.VMEM((nf, SUBLANES, tf), F32)]
    else:
        tm = _tile(T, 1024)
        assert tm % slen == 0
        tiles_per_seq = 1
        seq_per_tile = tm // slen
        cs_spec = pl.BlockSpec((seq_per_tile, SUBLANES, tf), lambda m, f: (m, 0, f))
        extra_in, extra_specs = [state], [cs_spec]
        scratch = []
    kern = functools.partial(_ffn_up_kernel, nseq=seq_per_tile, slen=tm // seq_per_tile,
                             tiles_per_seq=tiles_per_seq, carry=carry)
    n_tails = (T // tm) * seq_per_tile
    gated, tails = pl.pallas_call(
        kern,
        grid=(T // tm, nf),
        in_specs=[pl.BlockSpec((tm, D), lambda m, f: (m, 0)),
                  pl.BlockSpec((D, tf), lambda m, f: (0, f)),
                  pl.BlockSpec((D, tf), lambda m, f: (0, f + nf)),
                  pl.BlockSpec((3, tf), lambda m, f: (0, f)),
                  pl.BlockSpec((1, tf), lambda m, f: (0, f))] + extra_specs,
        out_specs=[pl.BlockSpec((tm, tf), lambda m, f: (m, f)), cs_spec],
        out_shape=[jax.ShapeDtypeStruct((T, F), BF16), jax.ShapeDtypeStruct((n_tails, SUBLANES, F), F32)],
        scratch_shapes=scratch,
        compiler_params=_cparams("arbitrary", "arbitrary"),
        name="ffn_up_carry" if carry else "ffn_up_state",
    )(xb, w_up, w_up, cw, cb, *extra_in)
    return gated, tails[tiles_per_seq - 1::tiles_per_seq]


def _hgrn_in_kernel(x_ref, wq_ref, wf_ref, wi_ref, wg_ref, lb_ref,
                    q_ref, k_ref, lf_ref, v_ref, sg_ref, *, layer, hh):
    x = x_ref[...]
    lbv = lb_ref[...]
    e = jnp.exp(lbv - jnp.max(lbv, axis=0, keepdims=True))
    p = e / jnp.sum(e, axis=0, keepdims=True)
    lb = jnp.sum(p[1:layer + 1, :], axis=0, keepdims=True)

    def put(ref, val):
        for j in range(hh):
            ref[j] = val[:, j * LANES:(j + 1) * LANES].astype(ref.dtype)

    put(q_ref, _dot(x, wq_ref[...]))
    f = lb + (1.0 - lb) * jax.nn.sigmoid(_dot(x, wf_ref[...]))
    put(lf_ref, jnp.log(f))
    put(k_ref, 1.0 - f)
    put(v_ref, _dot(x, wi_ref[...]))
    put(sg_ref, jax.nn.silu(_dot(x, wg_ref[...])))


def _hgrn_in(xb, w_in, b_lb, layer):
    T, D = xb.shape
    QK = w_in.shape[1] // 4
    H = QK // LANES
    tm, tn = _tile(T, 1024), _tile(QK, 256)
    nn = QK // tn
    hh = tn // LANES
    depth = b_lb.shape[0]
    w_specs = [pl.BlockSpec((D, tn), lambda m, n, k=k: (0, n + k * nn)) for k in range(4)]
    o_spec = pl.BlockSpec((hh, tm, LANES), lambda m, n: (n, m, 0))
    hm = lambda dt: jax.ShapeDtypeStruct((H, T, LANES), dt)
    return pl.pallas_call(
        functools.partial(_hgrn_in_kernel, layer=layer, hh=hh),
        grid=(T // tm, nn),
        in_specs=[pl.BlockSpec((tm, D), lambda m, n: (m, 0))] + w_specs
                 + [pl.BlockSpec((depth, tn), lambda m, n: (0, n))],
        out_specs=[o_spec] * 5,
        out_shape=[hm(BF16), hm(BF16), hm(F32), hm(BF16), hm(BF16)],
        compiler_params=_cparams("parallel", "arbitrary"),
        name="hgrn_in",
    )(xb, w_in, w_in, w_in, w_in, b_lb)


GLA_BLOCK = 32
GLA_MAX_SPREAD = 60.0
GLA_HEADS_PER_ITER = 8

def _nt_dot(a, b):
    return lax.dot_general(a, b, (((1,), (1,)), ((), ())), preferred_element_type=F32)


def _tn_dot(a, b):
    return lax.dot_general(a, b, (((0,), (0,)), ((), ())), preferred_element_type=F32)


def _gla_kernel(q_ref, k_ref, lf_ref, v_ref, sg_ref, gn_ref, *rest, C, hh, has_init):
    if has_init:
        s0_ref, o_ref, sout_ref, st_ref, acc_ref, bsc_ref, ops_ref = rest
    else:
        o_ref, sout_ref, st_ref, acc_ref, bsc_ref, ops_ref = rest
    t = pl.program_id(2)

    @pl.when(t == 0)
    def _():
        if has_init:
            def init(h, c):
                st_ref[h] = s0_ref[0, h].T
                return c
            lax.fori_loop(0, hh, init, 0)
        else:
            st_ref[...] = jnp.zeros(st_ref.shape, F32)

    row = lax.broadcasted_iota(jnp.int32, (C, C), 0)
    col = lax.broadcasted_iota(jnp.int32, (C, C), 1)
    tri = jnp.where(col <= row, 1.0, 0.0).astype(BF16)
    n8 = C // SUBLANES
    sub = lax.broadcasted_iota(jnp.int32, (1, SUBLANES, 1), 1)
    half = C // 2
    hrow = lax.broadcasted_iota(jnp.int32, (half, half), 0)
    hcol = lax.broadcasted_iota(jnp.int32, (half, half), 1)
    w0 = min(GLA_BLOCK, C)
    nb0 = C // w0
    dr = min(C, LANES)
    drow = lax.broadcasted_iota(jnp.int32, (dr, dr), 0)
    dcol = lax.broadcasted_iota(jnp.int32, (dr, dr), 1)

    def widths(lo, hi):
        out = []
        while lo < hi:
            out.append(lo)
            lo *= 2
        return out

    def halves(x, w, which):
        return jnp.concatenate([x[(2 * p + which) * w:(2 * p + which + 1) * w] for p in range(C // (2 * w))], axis=0)

    def rows_at(x, first, step, reps):
        return jnp.concatenate(
            [jnp.broadcast_to(x[i:i + 1, :], (reps, LANES)) for i in range(first, C, step)], axis=0)

    def load(h):
        q = q_ref[h].astype(F32)
        k = k_ref[h].astype(F32)
        vb = v_ref[h]
        g = lf_ref[h]
        g1 = g.astype(BF16)
        r1 = g - g1.astype(F32)
        g2 = r1.astype(BF16)
        g3 = (r1 - g2.astype(F32)).astype(BF16)
        b = _dot(tri, g1) + _dot(tri, g2) + _dot(tri, g3)
        return q, k, vb, vb.astype(F32), b

    def level(w, q, k, v32, b):
        bref = rows_at(b, w - 1, 2 * w, w)
        qt = (halves(q, w, 1) * jnp.exp(halves(b, w, 1) - bref)).astype(BF16)
        ks = (halves(k, w, 0) * jnp.exp(bref - halves(b, w, 0))).astype(BF16)
        att = _nt_dot(qt, ks)
        if C > 2 * w:
            att = jnp.where(_block_id(hrow, w) == _block_id(hcol, w), att, 0.0)
        return _dot(att.astype(BF16), halves(v32, w, 0).astype(BF16))

    def finish(h, o):
        o = o * lax.rsqrt(jnp.mean(o * o, axis=-1, keepdims=True) + RMS_EPS)
        o_ref[h] = (o * gn_ref[h] * sg_ref[h].astype(F32)).astype(o_ref.dtype)

    levels = widths(w0, C)
    Q_STATE, K_STATE, Q_BLOCK, K_BLOCK = 0, 1, 2, 3
    q_level = lambda l: 4 + 2 * l
    k_level = lambda l: 5 + 2 * l

    def prepare(h, slot):
        g = lf_ref[h]
        g1 = g.astype(BF16)
        r1 = g - g1.astype(F32)
        g2 = r1.astype(BF16)
        g3 = (r1 - g2.astype(F32)).astype(BF16)
        bsc_ref[slot] = _dot(tri, g1) + _dot(tri, g2) + _dot(tri, g3)
        b_end = bsc_ref[slot, C - 1:C, :]
        spread = None
        for i in range(nb0):
            rs = slice(i * w0, (i + 1) * w0)
            bc = bsc_ref[slot, rs, :]
            bf, bl = bc[0:1, :], bc[w0 - 1:w0, :]
            qp = q_ref[h, rs, :].astype(F32) * jnp.exp(bc - bf)
            kc = k_ref[h, rs, :].astype(F32)
            kq = kc * jnp.exp(bl - bc)
            ops_ref[slot, Q_STATE, rs, :] = (qp * jnp.exp(bf)).astype(BF16)
            ops_ref[slot, K_STATE, rs, :] = (kq * jnp.exp(b_end - bl)).astype(BF16)
            ops_ref[slot, Q_BLOCK, rs, :] = qp.astype(BF16)
            ops_ref[slot, K_BLOCK, rs, :] = (kc * jnp.exp(jnp.minimum(bf - bc, GLA_MAX_SPREAD))).astype(BF16)
            for l, w in enumerate(levels):
                p, off = divmod(i * w0, 2 * w)
                ref_row = p * 2 * w + w - 1
                br = bsc_ref[slot, ref_row:ref_row + 1, :]
                if off >= w:
                    dst = slice(p * w + off - w, p * w + off - w + w0)
                    ops_ref[slot, q_level(l), dst, :] = (qp * jnp.exp(bf - br)).astype(BF16)
                else:
                    dst = slice(p * w + off, p * w + off + w0)
                    ops_ref[slot, k_level(l), dst, :] = (kq * jnp.exp(br - bl)).astype(BF16)
            spread = bf - bl if spread is None else jnp.maximum(spread, bf - bl)
        return jnp.max(spread)

    def attend(h, slot):
        vb = v_ref[h]
        st = st_ref[h]
        o_state = _nt_dot(ops_ref[slot, Q_STATE], st.astype(BF16))
        acc_ref[slot] = o_state
        st_new = st * jnp.exp(bsc_ref[slot, C - 1:C, :]) + _tn_dot(vb, ops_ref[slot, K_STATE])
        st_ref[h] = st_new
        sout_ref[0, h] = st_new.T
        blk = [o_state[i * w0:(i + 1) * w0] for i in range(nb0)]
        for l, w in enumerate(levels):
            att = _nt_dot(ops_ref[slot, q_level(l), 0:half, :], ops_ref[slot, k_level(l), 0:half, :])
            if C > 2 * w:
                att = jnp.where(_block_id(hrow, w) == _block_id(hcol, w), att, 0.0)
            ot = _dot(att.astype(BF16), halves(vb, w, 0))
            for p in range(C // (2 * w)):
                for i in range(w // w0):
                    blk[(2 * p + 1) * w // w0 + i] += ot[p * w + i * w0:p * w + (i + 1) * w0]
        keep = (_block_id(drow, w0) == _block_id(dcol, w0)) & (dcol <= drow)
        for r in range(C // dr):
            rs = slice(r * dr, (r + 1) * dr)
            att = jnp.where(keep, _nt_dot(ops_ref[slot, Q_BLOCK, rs, :], ops_ref[slot, K_BLOCK, rs, :]), 0.0)
            od = _dot(att.astype(BF16), vb[rs])
            for i in range(dr // w0):
                blk[r * dr // w0 + i] += od[i * w0:(i + 1) * w0]
        finish(h, jnp.concatenate(blk, axis=0))

    def common(hs):
        spreads = [prepare(h, slot) for slot, h in enumerate(hs)]
        for slot, h in enumerate(hs):
            attend(h, slot)
        return spreads

    def exact(h, slot):
        q, k, _, v32, b = load(h)
        for w in widths(SUBLANES, C):
            ot = level(w, q, k, v32, b)
            for p in range(C // (2 * w)):
                acc_ref[slot, (2 * p + 1) * w:(2 * p + 2) * w, :] += ot[p * w:(p + 1) * w, :]
        b3 = b.reshape(n8, SUBLANES, LANES)
        q3 = q.reshape(n8, SUBLANES, LANES)
        k3 = k.reshape(n8, SUBLANES, LANES)
        v3 = v32.reshape(n8, SUBLANES, LANES)
        o3 = jnp.zeros((n8, SUBLANES, LANES), F32)
        for s in range(SUBLANES):
            d = jnp.where(sub >= s, b3 - b3[:, s:s + 1, :], NEG_BIG)
            e = q3 * jnp.exp(d) * k3[:, s:s + 1, :]
            o3 = o3 + jnp.sum(e, axis=-1, keepdims=True) * v3[:, s:s + 1, :]
        finish(h, acc_ref[slot] + o3.reshape(C, LANES))

    per_iter = acc_ref.shape[0]
    assert hh % per_iter == 0

    def heads(i, c):
        hs = [i * per_iter + j for j in range(per_iter)]
        spreads = common(hs)
        for slot, (h, spread) in enumerate(zip(hs, spreads)):
            @pl.when(jnp.logical_not(spread <= GLA_MAX_SPREAD))
            def _(h=h, slot=slot):
                exact(h, slot)
        return c

    lax.fori_loop(0, hh // per_iter, heads, 0)


def _gla(q, k, lf, v, sg, gn, nseq, slen, s0):
    H, T, _ = q.shape
    C = _tile(slen, 256)
    tiles = slen // C
    n_levels = _log2(C // min(GLA_BLOCK, C))
    hh = _tile(H * LANES, 16 * LANES) // LANES
    per_iter = math.gcd(hh, GLA_HEADS_PER_ITER)
    row_spec = pl.BlockSpec((hh, C, LANES), lambda s, g, t: (g, s * tiles + t, 0))
    st_spec = pl.BlockSpec((1, hh, LANES, LANES), lambda s, g, t: (s, g, 0, 0))
    in_specs = [row_spec] * 5 + [pl.BlockSpec((hh, 1, LANES), lambda s, g, t: (g, 0, 0))]
    args = [q, k, lf, v, sg, gn]
    if s0 is not None:
        in_specs.append(st_spec)
        args.append(s0)
    return pl.pallas_call(
        functools.partial(_gla_kernel, C=C, hh=hh, has_init=s0 is not None),
        grid=(nseq, H // hh, tiles),
        in_specs=in_specs,
        out_specs=[row_spec, st_spec],
        out_shape=[jax.ShapeDtypeStruct((H, T, LANES), BF16),
                   jax.ShapeDtypeStruct((nseq, H, LANES, LANES), F32)],
        scratch_shapes=[pltpu.VMEM((hh, LANES, LANES), F32),
                        pltpu.VMEM((per_iter, C, LANES), F32),
                        pltpu.VMEM((per_iter, C, LANES), F32),
                        pltpu.VMEM((per_iter, 4 + 2 * n_levels, C, LANES), BF16)],
        compiler_params=_cparams("parallel", "parallel", "arbitrary"),
        name="gla_init" if s0 is not None else "gla",
    )(*args)


PREP_ROWS = 64
PREP_BLOCK = 256


def _cast_pad_halves_kernel(w_ref, o_ref):
    n = w_ref.shape[1]
    o_ref[:, :n] = w_ref[...].astype(o_ref.dtype)
    if o_ref.shape[1] > n:
        o_ref[:, n:] = jnp.zeros((o_ref.shape[0], o_ref.shape[1] - n), o_ref.dtype)


def _cast_pad_halves(w, layer, pad):
    _, D, F2 = w.shape
    F = F2 // 2
    tr = _tile(D, PREP_ROWS)
    return pl.pallas_call(
        _cast_pad_halves_kernel,
        grid=(D // tr, 2),
        in_specs=[pl.BlockSpec((None, tr, F), lambda r, h: (layer, r, h))],
        out_specs=pl.BlockSpec((tr, F + pad), lambda r, h: (r, h)),
        out_shape=jax.ShapeDtypeStruct((D, 2 * (F + pad)), BF16),
        compiler_params=_cparams("parallel", "parallel"),
        name="cast_pad_halves",
    )(w)


def _cast_pad_rows_kernel(w_ref, o_ref, *, nb):
    r = pl.program_id(0)
    o_ref[...] = jnp.where(r < nb, w_ref[...], 0.0).astype(o_ref.dtype)


def _cast_pad_rows(w, layer, pad):
    _, F, D = w.shape
    rb = PREP_BLOCK
    assert F % rb == 0 and pad % rb == 0
    nb = F // rb
    return pl.pallas_call(
        functools.partial(_cast_pad_rows_kernel, nb=nb),
        grid=((F + pad) // rb,),
        in_specs=[pl.BlockSpec((None, rb, D), lambda r: (layer, jnp.minimum(r, nb - 1), 0))],
        out_specs=pl.BlockSpec((rb, D), lambda r: (r, 0)),
        out_shape=jax.ShapeDtypeStruct((F + pad, D), BF16),
        compiler_params=_cparams("parallel"),
        name="cast_pad_rows",
    )(w)


FF_ALIGN = 512


def kernel(x_prompt, x_sample, state_hgrn, cache_conv, a_w_in, a_ln_g, a_ln_b, a_w_s, a_b_s, a_w_out,
           b_w_in, b_lb, b_gn_g, b_w_out, f_w_up, f_conv_w, f_conv_b, f_w_down, ln1_g, ln1_b, ln2_g, ln2_b):
    depth = ln1_g.shape[0]
    alpha = (2 * depth) ** 0.25
    bp, lp, D = x_prompt.shape
    bs, ls, _ = x_sample.shape
    blk = a_w_s.shape[-1]
    assert blk == LANES and lp % blk == 0 and blk % ls == 0 and (bs * ls) % blk == 0
    d_ff = f_conv_w.shape[-1]
    f_pad = -d_ff % FF_ALIGN
    H = b_gn_g.shape[-1] // LANES

    xp = x_prompt.reshape(bp * lp, D)
    xs = x_sample.reshape(bs * ls, D)
    groups = [[xp, xp.astype(BF16), bp, lp], [xs, xs.astype(BF16), bs, ls]]

    v_rows, s_prompt, s_sample, c_prompt, c_sample = [], [], [], [], []
    for i in range(depth):
        j = i // 2
        row = lambda a: a[i][None, :]
        if i % 2 == 0:
            w_in = a_w_in[j].astype(BF16)
            w_out = a_w_out[j].astype(BF16)
            ln_g, ln_b = a_ln_g[j][None, :], a_ln_b[j][None, :]
            reps = blk // ls
            ws_variants = [a_w_s[j], jnp.tile(a_w_s[j][:, :ls, :ls], (1, reps, reps))]
            bs_variants = [a_b_s[j].T, jnp.tile(a_b_s[j][:, :ls], (1, reps)).T]
            for gi, grp in enumerate(groups):
                xf, xb, nseq, slen = grp
                u, v = _a_in(xb, w_in)
                gated, vn = _spatial(u, v, ln_g, ln_b, ws_variants[gi], bs_variants[gi],
                                     period=min(slen, blk), want_vn=gi == 1)
                if gi == 1:
                    v_rows.append(vn.reshape(nseq, slen, -1))
                grp[0], grp[1] = _mm_res_ln(gated, w_out, xf, row(ln1_g), row(ln1_b), alpha, head_major=False)
        else:
            w_in = b_w_in[j].astype(BF16)
            w_out = b_w_out[j].astype(BF16)
            gn = b_gn_g[j].reshape(H, 1, LANES)
            for gi, grp in enumerate(groups):
                xf, xb, nseq, slen = grp
                q, k, lf, v, sg = _hgrn_in(xb, w_in, b_lb, i)
                o, s_fin = _gla(q, k, lf, v, sg, gn, nseq, slen, state_hgrn[j] if gi == 1 else None)
                (s_sample if gi == 1 else s_prompt).append(s_fin)
                grp[0], grp[1] = _mm_res_ln(o, w_out, xf, row(ln1_g), row(ln1_b), alpha, head_major=True)

        pad_cols = lambda a: jnp.pad(a, ((0, 0), (0, f_pad)))
        w_up = _cast_pad_halves(f_w_up, i, f_pad)
        cw = pad_cols(f_conv_w[i])
        cb = pad_cols(f_conv_b[i][None, :])
        wd = _cast_pad_rows(f_w_down, i, f_pad)
        for gi, grp in enumerate(groups):
            xf, xb, nseq, slen = grp
            state = None
            if gi == 1:
                state = jnp.pad(cache_conv[i], ((0, 0), (SUBLANES - 2, 0), (0, f_pad)))
            gated, tail = _ffn_up(xb, w_up, cw, cb, nseq, slen, state)
            (c_sample if gi == 1 else c_prompt).append(tail[:, SUBLANES - 2:, :d_ff])
            grp[0], grp[1] = _mm_res_ln(gated, wd, xf, row(ln2_g), row(ln2_b), alpha, head_major=False)

    return (groups[0][0].reshape(bp, lp, D), groups[1][0].reshape(bs, ls, D), jnp.stack(v_rows),
            jnp.stack(s_prompt), jnp.stack(s_sample), jnp.stack(c_prompt), jnp.stack(c_sample))
```

```python
import functools
import math

import jax
import jax.numpy as jnp
from jax import lax
from jax.experimental import pallas as pl
from jax.experimental.pallas import tpu as pltpu

F32 = jnp.float32
BF16 = jnp.bfloat16

LN_EPS = 1e-5
RMS_EPS = 1e-6
LANES = 128
SUBLANES = 8
VMEM_LIMIT_BYTES = 56 * 1024 * 1024
NEG_BIG = -1e30


def _cparams(*sem):
    return pltpu.CompilerParams(dimension_semantics=sem, vmem_limit_bytes=VMEM_LIMIT_BYTES)


def _tile(n, pref):
    t = min(n, pref)
    while n % t:
        t -= LANES
    return t


def _dot(a, b):
    return jnp.dot(a, b, preferred_element_type=F32)


def _log2(n):
    assert n > 0 and n & (n - 1) == 0, f"{n} is not a power of two"
    return n.bit_length() - 1


def _block_id(idx, size):
    return lax.shift_right_logical(idx, _log2(size))


def _layer_norm_rows(x, g, b):
    mu = jnp.mean(x, axis=-1, keepdims=True)
    xc = x - mu
    var = jnp.mean(xc * xc, axis=-1, keepdims=True)
    return xc * lax.rsqrt(var + LN_EPS) * g + b


A_SUB_COLS = 256


def _a_in_kernel(x_ref, wu_ref, wv_ref, u_ref, v_ref):
    x = x_ref[...]
    sub = _tile(u_ref.shape[1], A_SUB_COLS)
    for w_ref, o_ref in ((wu_ref, u_ref), (wv_ref, v_ref)):
        for j in range(u_ref.shape[1] // sub):
            cs = slice(j * sub, (j + 1) * sub)
            o_ref[:, cs] = jax.nn.gelu(_dot(x, w_ref[:, cs])).astype(o_ref.dtype)


def _a_in(xb, w_in):
    T, D = xb.shape
    W = w_in.shape[1] // 2
    tm, tn = _tile(T, 1024), _tile(W, 512)
    nn = W // tn
    return pl.pallas_call(
        _a_in_kernel,
        grid=(T // tm, nn),
        in_specs=[pl.BlockSpec((tm, D), lambda m, n: (m, 0)),
                  pl.BlockSpec((D, tn), lambda m, n: (0, n)),
                  pl.BlockSpec((D, tn), lambda m, n: (0, n + nn))],
        out_specs=[pl.BlockSpec((tm, tn), lambda m, n: (m, n)),
                   pl.BlockSpec((tm, tn), lambda m, n: (m, n))],
        out_shape=[jax.ShapeDtypeStruct((T, W), BF16), jax.ShapeDtypeStruct((T, W), F32)],
        compiler_params=_cparams("parallel", "arbitrary"),
        name="a_in",
    )(xb, w_in, w_in)


def _spatial_kernel(u_ref, v_ref, g_ref, b_ref, ws_ref, bs_ref, o_ref, vn_ref, *, nb, groups, period):
    vn_ref[...] = _layer_norm_rows(v_ref[...], g_ref[...], b_ref[...])
    row = lax.broadcasted_iota(jnp.int32, (LANES, LANES), 0)
    col = lax.broadcasted_iota(jnp.int32, (LANES, LANES), 1)
    keep = (col <= row) & (_block_id(row, period) == _block_id(col, period))
    for g in range(groups):
        cs = slice(g * LANES, (g + 1) * LANES)
        wsg = jnp.where(keep, ws_ref[g], 0.0).astype(BF16)
        bias = bs_ref[:, g:g + 1]
        for j in range(nb):
            rs = slice(j * LANES, (j + 1) * LANES)
            mixed = _dot(wsg, vn_ref[rs, cs].astype(BF16)) + bias
            o_ref[rs, cs] = (u_ref[rs, cs].astype(F32) * mixed).astype(o_ref.dtype)


def _spatial(u, v, ln_g, ln_b, ws, bs_t, period, want_vn):
    T, W = v.shape
    G = ws.shape[0]
    tb = _tile(T, 2 * LANES)
    row_spec = pl.BlockSpec((tb, W), lambda i: (i, 0))
    vec_spec = pl.BlockSpec((1, W), lambda i: (0, 0))
    kern = functools.partial(_spatial_kernel, nb=tb // LANES, groups=G, period=period)
    common = dict(
        grid=(T // tb,),
        in_specs=[row_spec, row_spec, vec_spec, vec_spec,
                  pl.BlockSpec((G, LANES, LANES), lambda i: (0, 0, 0)),
                  pl.BlockSpec((LANES, G), lambda i: (0, 0))],
        compiler_params=_cparams("parallel"),
    )
    if want_vn:
        return pl.pallas_call(
            kern, out_specs=[row_spec, row_spec],
            out_shape=[jax.ShapeDtypeStruct((T, W), BF16), jax.ShapeDtypeStruct((T, W), F32)],
            name="spatial_vn", **common)(u, v, ln_g, ln_b, ws, bs_t)
    gated = pl.pallas_call(
        kern, out_specs=row_spec, out_shape=jax.ShapeDtypeStruct((T, W), BF16),
        scratch_shapes=[pltpu.VMEM((tb, W), F32)],
        name="spatial", **common)(u, v, ln_g, ln_b, ws, bs_t)
    return gated, None


LN_ROWS = 64
LN_COLS = 1024
MM_K_TILE = 2816


def _ln_inplace(acc_ref, g_ref, b_ref, ob_ref, after_chunk):
    rows, n = acc_ref.shape
    cw = _tile(n, LN_COLS)
    slabs = [slice(j * cw, (j + 1) * cw) for j in range(n // cw)]

    def body(i, c):
        rs = pl.ds(pl.multiple_of(i * LN_ROWS, LN_ROWS), LN_ROWS)
        mu = sum(jnp.sum(acc_ref[rs, cs], axis=-1, keepdims=True) for cs in slabs) * (1.0 / n)
        var = sum(jnp.sum(jnp.square(acc_ref[rs, cs] - mu), axis=-1, keepdims=True) for cs in slabs) * (1.0 / n)
        r = lax.rsqrt(var + LN_EPS)
        for cs in slabs:
            y = (acc_ref[rs, cs] - mu) * r * g_ref[:, cs] + b_ref[:, cs]
            acc_ref[rs, cs] = y
            ob_ref[rs, cs] = y.astype(ob_ref.dtype)
        after_chunk(i)
        return c

    lax.fori_loop(0, rows // LN_ROWS, body, 0)


def _mm_res_ln_kernel(x_ref, w_ref, res_ref, g_ref, b_ref, of_hbm, ob_hbm, acc_ref, ob_ref, sem, *scratch,
                      alpha, nm, nk, nn, tn, heads):
    m = pl.program_id(0)
    k = pl.program_id(1)
    n = pl.program_id(2)
    tm = acc_ref.shape[0]
    n_chunks = tm // LN_ROWS

    def copies(tile, i):
        src = pl.ds(pl.multiple_of(i * LN_ROWS, LN_ROWS), LN_ROWS)
        dst = pl.ds(pl.multiple_of(tile * tm + i * LN_ROWS, LN_ROWS), LN_ROWS)
        return (pltpu.make_async_copy(acc_ref.at[src], of_hbm.at[dst], sem.at[0]),
                pltpu.make_async_copy(ob_ref.at[src], ob_hbm.at[dst], sem.at[1]))

    def start_chunk(i):
        for cp in copies(m, i):
            cp.start()

    def wait_tile(tile):
        def body(i, c):
            for cp in copies(tile, i):
                cp.wait()
            return c
        lax.fori_loop(0, n_chunks, body, 0)

    @pl.when((m > 0) & (k == 0) & (n == 0))
    def _():
        wait_tile(m - 1)

    if heads:
        xs_ref, = scratch

        @pl.when(n == 0)
        def _():
            for h in range(heads):
                xs_ref[:, h * LANES:(h + 1) * LANES] = x_ref[h]

        x = xs_ref[...]
    else:
        x = x_ref[...]
    cols = pl.ds(pl.multiple_of(n * tn, tn), tn)

    @pl.when(k == 0)
    def _():
        acc_ref[:, cols] = alpha * res_ref[...]

    acc_ref[:, cols] += _dot(x, w_ref[...])

    @pl.when((k == nk - 1) & (n == nn - 1))
    def _():
        _ln_inplace(acc_ref, g_ref, b_ref, ob_ref, start_chunk)

    @pl.when((m == nm - 1) & (k == nk - 1) & (n == nn - 1))
    def _():
        wait_tile(m)


def _mm_res_ln(x, w, res, ln_g, ln_b, alpha, head_major):
    T, N = res.shape
    K = w.shape[0]
    tm, tn = _tile(T, 1024), _tile(N, 512)
    tk = K if K <= 4096 else _tile(K, MM_K_TILE)
    nk, nn = K // tk, N // tn
    heads = K // LANES if head_major else 0
    once = dict(pipeline_mode=pl.Buffered(1))
    if head_major:
        assert nk == 1
        x_spec = pl.BlockSpec((heads, tm, LANES), lambda m, k, n: (0, m, 0), **once)
        scratch = [pltpu.VMEM((tm, K), BF16)]
    elif nk == 1:
        x_spec = pl.BlockSpec((tm, K), lambda m, k, n: (m, 0), **once)
        scratch = []
    else:
        x_spec = pl.BlockSpec((tm, tk), lambda m, k, n: (m, k))
        scratch = []
    vec_spec = pl.BlockSpec((1, N), lambda m, k, n: (0, 0))
    res_spec = pl.BlockSpec((tm, tn), lambda m, k, n: (m, jnp.where(k == 0, n, nn - 1)))
    hbm_spec = pl.BlockSpec(memory_space=pl.ANY)
    assert tm % LN_ROWS == 0
    return pl.pallas_call(
        functools.partial(_mm_res_ln_kernel, alpha=alpha, nm=T // tm, nk=nk, nn=nn, tn=tn, heads=heads),
        grid=(T // tm, nk, nn),
        in_specs=[x_spec,
                  pl.BlockSpec((tk, tn), lambda m, k, n: (k, n)),
                  res_spec, vec_spec, vec_spec],
        out_specs=[hbm_spec, hbm_spec],
        out_shape=[jax.ShapeDtypeStruct((T, N), F32), jax.ShapeDtypeStruct((T, N), BF16)],
        scratch_shapes=[pltpu.VMEM((tm, N), F32), pltpu.VMEM((tm, N), BF16),
                        pltpu.SemaphoreType.DMA((2,))] + scratch,
        compiler_params=_cparams("arbitrary", "arbitrary", "arbitrary"),
        name="mm_res_ln",
    )(x, w, res, ln_g, ln_b)


FFN_SUB_COLS = 256


def _ffn_up_kernel(x_ref, wa_ref, wu_ref, cw_ref, cb_ref, *rest, nseq, slen, tiles_per_seq, carry):
    if carry:
        g_ref, cs_ref, halo_ref = rest
    else:
        st_ref, g_ref, cs_ref = rest
    m = pl.program_id(0)
    f = pl.program_id(1)
    tm, tf = g_ref.shape
    sub = _tile(tf, FFN_SUB_COLS)

    if carry:
        @pl.when(m % tiles_per_seq == 0)
        def _():
            halo_ref[f] = jnp.zeros((SUBLANES, tf), F32)

    _log2(slen)
    pos = lax.broadcasted_iota(jnp.int32, (tm, 1), 0) & (slen - 1)
    x = x_ref[...]
    acts = []
    for j in range(tf // sub):
        cs = slice(j * sub, (j + 1) * sub)
        a = _dot(x, wa_ref[:, cs])
        if carry:
            st = halo_ref[f, :, cs][None]
        else:
            st = st_ref[:, :, cs]

        def expand(rows):
            return jnp.broadcast_to(rows, (nseq, slen, sub)).reshape(tm, sub)

        s_m1 = expand(st[:, SUBLANES - 1:SUBLANES, :])
        s_m2 = expand(st[:, SUBLANES - 2:SUBLANES - 1, :])
        prev1 = jnp.where(pos == 0, s_m1, pltpu.roll(a, 1, 0))
        prev2 = jnp.where(pos == 0, s_m2, jnp.where(pos == 1, s_m1, pltpu.roll(a, 2, 0)))
        cw = cw_ref[:, cs]
        c = cb_ref[:, cs] + prev2 * cw[0:1, :] + prev1 * cw[1:2, :] + a * cw[2:3, :]

        tail = a.reshape(nseq, slen, sub)[:, slen - SUBLANES:, :]
        cs_ref[:, :, cs] = tail
        if carry:
            halo_ref[f, :, cs] = tail[0]
        acts.append(jax.nn.silu(c))
    for j, act in enumerate(acts):
        cs = slice(j * sub, (j + 1) * sub)
        g_ref[:, cs] = (act * _dot(x, wu_ref[:, cs])).astype(g_ref.dtype)


def _ffn_up(xb, w_up, cw, cb, nseq, slen, state):
    T, D = xb.shape
    F = w_up.shape[1] // 2
    tf = _tile(F, 512)
    nf = F // tf
    carry = state is None
    if carry:
        tm = _tile(slen, 1024)
        tiles_per_seq = slen // tm
        seq_per_tile = 1
        cs_spec = pl.BlockSpec((1, SUBLANES, tf), lambda m, f: (m, 0, f))
        extra_in, extra_specs = [], []
        scratch = [pltpu.VMEM((nf, SUBLANES, tf), F32)]
    else:
        tm = _tile(T, 1024)
        assert tm % slen == 0
        tiles_per_seq = 1
        seq_per_tile = tm // slen
        cs_spec = pl.BlockSpec((seq_per_tile, SUBLANES, tf), lambda m, f: (m, 0, f))
        extra_in, extra_specs = [state], [cs_spec]
        scratch = []
    kern = functools.partial(_ffn_up_kernel, nseq=seq_per_tile, slen=tm // seq_per_tile,
                             tiles_per_seq=tiles_per_seq, carry=carry)
    n_tails = (T // tm) * seq_per_tile
    gated, tails = pl.pallas_call(
        kern,
        grid=(T // tm, nf),
        in_specs=[pl.BlockSpec((tm, D), lambda m, f: (m, 0)),
                  pl.BlockSpec((D, tf), lambda m, f: (0, f)),
                  pl.BlockSpec((D, tf), lambda m, f: (0, f + nf)),
                  pl.BlockSpec((3, tf), lambda m, f: (0, f)),
                  pl.BlockSpec((1, tf), lambda m, f: (0, f))] + extra_specs,
        out_specs=[pl.BlockSpec((tm, tf), lambda m, f: (m, f)), cs_spec],
        out_shape=[jax.ShapeDtypeStruct((T, F), BF16), jax.ShapeDtypeStruct((n_tails, SUBLANES, F), F32)],
        scratch_shapes=scratch,
        compiler_params=_cparams("arbitrary", "arbitrary"),
        name="ffn_up_carry" if carry else "ffn_up_state",
    )(xb, w_up, w_up, cw, cb, *extra_in)
    return gated, tails[tiles_per_seq - 1::tiles_per_seq]


def _hgrn_in_kernel(x_ref, wq_ref, wf_ref, wi_ref, wg_ref, lb_ref,
                    q_ref, k_ref, lf_ref, v_ref, sg_ref, *, layer, hh):
    x = x_ref[...]
    lbv = lb_ref[...]
    e = jnp.exp(lbv - jnp.max(lbv, axis=0, keepdims=True))
    p = e / jnp.sum(e, axis=0, keepdims=True)
    lb = jnp.sum(p[1:layer + 1, :], axis=0, keepdims=True)

    def put(ref, val):
        for j in range(hh):
            ref[j] = val[:, j * LANES:(j + 1) * LANES].astype(ref.dtype)

    put(q_ref, _dot(x, wq_ref[...]))
    f = lb + (1.0 - lb) * jax.nn.sigmoid(_dot(x, wf_ref[...]))
    put(lf_ref, jnp.log(f))
    put(k_ref, 1.0 - f)
    put(v_ref, _dot(x, wi_ref[...]))
    put(sg_ref, jax.nn.silu(_dot(x, wg_ref[...])))


def _hgrn_in(xb, w_in, b_lb, layer):
    T, D = xb.shape
    QK = w_in.shape[1] // 4
    H = QK // LANES
    tm, tn = _tile(T, 1024), _tile(QK, 256)
    nn = QK // tn
    hh = tn // LANES
    depth = b_lb.shape[0]
    w_specs = [pl.BlockSpec((D, tn), lambda m, n, k=k: (0, n + k * nn)) for k in range(4)]
    o_spec = pl.BlockSpec((hh, tm, LANES), lambda m, n: (n, m, 0))
    hm = lambda dt: jax.ShapeDtypeStruct((H, T, LANES), dt)
    return pl.pallas_call(
        functools.partial(_hgrn_in_kernel, layer=layer, hh=hh),
        grid=(T // tm, nn),
        in_specs=[pl.BlockSpec((tm, D), lambda m, n: (m, 0))] + w_specs
                 + [pl.BlockSpec((depth, tn), lambda m, n: (0, n))],
        out_specs=[o_spec] * 5,
        out_shape=[hm(BF16), hm(BF16), hm(F32), hm(BF16), hm(BF16)],
        compiler_params=_cparams("parallel", "arbitrary"),
        name="hgrn_in",
    )(xb, w_in, w_in, w_in, w_in, b_lb)


GLA_BLOCK = 32
GLA_MAX_SPREAD = 60.0
GLA_HEADS_PER_ITER = 8

def _nt_dot(a, b):
    return lax.dot_general(a, b, (((1,), (1,)), ((), ())), preferred_element_type=F32)


def _tn_dot(a, b):
    return lax.dot_general(a, b, (((0,), (0,)), ((), ())), preferred_element_type=F32)


def _gla_kernel(q_ref, k_ref, lf_ref, v_ref, sg_ref, gn_ref, *rest, C, hh, has_init, one_tile):
    if has_init:
        s0_ref, o_ref, sout_ref, st_ref, acc_ref, ops_ref = rest
    else:
        o_ref, sout_ref, st_ref, acc_ref, ops_ref = rest
    t = pl.program_id(2)
    direct_state = has_init and one_tile

    if not direct_state:
        @pl.when(t == 0)
        def _():
            if has_init:
                def init(h, c):
                    st_ref[h] = s0_ref[0, h].T
                    return c
                lax.fori_loop(0, hh, init, 0)
            else:
                st_ref[...] = jnp.zeros(st_ref.shape, F32)

    row = lax.broadcasted_iota(jnp.int32, (C, C), 0)
    col = lax.broadcasted_iota(jnp.int32, (C, C), 1)
    tri = jnp.where(col <= row, 1.0, 0.0).astype(BF16)
    n8 = C // SUBLANES
    sub = lax.broadcasted_iota(jnp.int32, (1, SUBLANES, 1), 1)
    half = C // 2
    hrow = lax.broadcasted_iota(jnp.int32, (half, half), 0)
    hcol = lax.broadcasted_iota(jnp.int32, (half, half), 1)
    w0 = min(GLA_BLOCK, C)
    nb0 = C // w0
    tri0 = tri[:w0, :w0]
    dr = min(C, LANES)
    drow = lax.broadcasted_iota(jnp.int32, (dr, dr), 0)
    dcol = lax.broadcasted_iota(jnp.int32, (dr, dr), 1)

    def widths(lo, hi):
        out = []
        while lo < hi:
            out.append(lo)
            lo *= 2
        return out

    def halves(x, w, which):
        return jnp.concatenate([x[(2 * p + which) * w:(2 * p + which + 1) * w] for p in range(C // (2 * w))], axis=0)

    def rows_at(x, first, step, reps):
        return jnp.concatenate(
            [jnp.broadcast_to(x[i:i + 1, :], (reps, LANES)) for i in range(first, C, step)], axis=0)

    def load(h):
        q = q_ref[h].astype(F32)
        k = k_ref[h].astype(F32)
        vb = v_ref[h]
        g = lf_ref[h]
        g1 = g.astype(BF16)
        r1 = g - g1.astype(F32)
        g2 = r1.astype(BF16)
        g3 = (r1 - g2.astype(F32)).astype(BF16)
        b = _dot(tri, g1) + _dot(tri, g2) + _dot(tri, g3)
        return q, k, vb, vb.astype(F32), b

    def level(w, q, k, v32, b):
        bref = rows_at(b, w - 1, 2 * w, w)
        qt = (halves(q, w, 1) * jnp.exp(halves(b, w, 1) - bref)).astype(BF16)
        ks = (halves(k, w, 0) * jnp.exp(bref - halves(b, w, 0))).astype(BF16)
        att = _nt_dot(qt, ks)
        if C > 2 * w:
            att = jnp.where(_block_id(hrow, w) == _block_id(hcol, w), att, 0.0)
        return _dot(att.astype(BF16), halves(v32, w, 0).astype(BF16))

    def finish(h, o):
        o = o * lax.rsqrt(jnp.mean(o * o, axis=-1, keepdims=True) + RMS_EPS)
        o_ref[h] = (o * gn_ref[h] * sg_ref[h].astype(F32)).astype(o_ref.dtype)

    levels = widths(w0, C)
    Q_STATE, K_STATE, Q_BLOCK, K_BLOCK = 0, 1, 2, 3
    q_level = lambda l: 4 + 2 * l
    k_level = lambda l: 5 + 2 * l

    def prepare(h, slot):
        g = lf_ref[h]
        gw = jnp.concatenate([g[i * w0:(i + 1) * w0] for i in range(nb0)], axis=1)
        g1 = gw.astype(BF16)
        r1 = gw - g1.astype(F32)
        g2 = r1.astype(BF16)
        g3 = (r1 - g2.astype(F32)).astype(BF16)
        cw = _dot(tri0, g1) + _dot(tri0, g2) + _dot(tri0, g3)
        cbs = [cw[:, i * LANES:(i + 1) * LANES] for i in range(nb0)]
        b_first, b_last = [], []
        offset = None
        for cb in cbs:
            first, total = cb[0:1, :], cb[w0 - 1:w0, :]
            b_first.append(first if offset is None else offset + first)
            offset = total if offset is None else offset + total
            b_last.append(offset)
        b_end = offset
        spread = None
        for i, cb in enumerate(cbs):
            rs = slice(i * w0, (i + 1) * w0)
            first, total = cb[0:1, :], cb[w0 - 1:w0, :]
            bf, bl = b_first[i], b_last[i]
            qp = q_ref[h, rs, :].astype(F32) * jnp.exp(cb - first)
            kc = k_ref[h, rs, :].astype(F32)
            kq = kc * jnp.exp(total - cb)
            ops_ref[slot, Q_STATE, rs, :] = (qp * jnp.exp(bf)).astype(BF16)
            ops_ref[slot, K_STATE, rs, :] = (kq * jnp.exp(b_end - bl)).astype(BF16)
            ops_ref[slot, Q_BLOCK, rs, :] = qp.astype(BF16)
            ops_ref[slot, K_BLOCK, rs, :] = (kc * jnp.exp(jnp.minimum(first - cb, GLA_MAX_SPREAD))).astype(BF16)
            for l, w in enumerate(levels):
                p, off = divmod(i * w0, 2 * w)
                br = b_last[(p * 2 * w + w - 1) // w0]
                if off >= w:
                    dst = slice(p * w + off - w, p * w + off - w + w0)
                    ops_ref[slot, q_level(l), dst, :] = (qp * jnp.exp(bf - br)).astype(BF16)
                else:
                    dst = slice(p * w + off, p * w + off + w0)
                    ops_ref[slot, k_level(l), dst, :] = (kq * jnp.exp(br - bl)).astype(BF16)
            spread = first - total if spread is None else jnp.maximum(spread, first - total)
        return jnp.max(spread), b_end

    def attend(h, slot, b_end):
        vb = v_ref[h]
        st = s0_ref[0, h].T if direct_state else st_ref[h]
        o_state = _nt_dot(ops_ref[slot, Q_STATE], st.astype(BF16))
        acc_ref[slot] = o_state
        st_new = st * jnp.exp(b_end) + _tn_dot(vb, ops_ref[slot, K_STATE])
        if not direct_state:
            st_ref[h] = st_new
        sout_ref[0, h] = st_new.T
        blk = [o_state[i * w0:(i + 1) * w0] for i in range(nb0)]
        for l, w in enumerate(levels):
            att = _nt_dot(ops_ref[slot, q_level(l), 0:half, :], ops_ref[slot, k_level(l), 0:half, :])
            if C > 2 * w:
                att = jnp.where(_block_id(hrow, w) == _block_id(hcol, w), att, 0.0)
            ot = _dot(att.astype(BF16), halves(vb, w, 0))
            for p in range(C // (2 * w)):
                for i in range(w // w0):
                    blk[(2 * p + 1) * w // w0 + i] += ot[p * w + i * w0:p * w + (i + 1) * w0]
        keep = (_block_id(drow, w0) == _block_id(dcol, w0)) & (dcol <= drow)
        for r in range(C // dr):
            rs = slice(r * dr, (r + 1) * dr)
            att = jnp.where(keep, _nt_dot(ops_ref[slot, Q_BLOCK, rs, :], ops_ref[slot, K_BLOCK, rs, :]), 0.0)
            od = _dot(att.astype(BF16), vb[rs])
            for i in range(dr // w0):
                blk[r * dr // w0 + i] += od[i * w0:(i + 1) * w0]
        finish(h, jnp.concatenate(blk, axis=0))

    def common(hs):
        prepared = [prepare(h, slot) for slot, h in enumerate(hs)]
        for slot, (h, (_, b_end)) in enumerate(zip(hs, prepared)):
            attend(h, slot, b_end)
        return [spread for spread, _ in prepared]

    def exact(h, slot):
        q, k, _, v32, b = load(h)
        for w in widths(SUBLANES, C):
            ot = level(w, q, k, v32, b)
            for p in range(C // (2 * w)):
                acc_ref[slot, (2 * p + 1) * w:(2 * p + 2) * w, :] += ot[p * w:(p + 1) * w, :]
        b3 = b.reshape(n8, SUBLANES, LANES)
        q3 = q.reshape(n8, SUBLANES, LANES)
        k3 = k.reshape(n8, SUBLANES, LANES)
        v3 = v32.reshape(n8, SUBLANES, LANES)
        o3 = jnp.zeros((n8, SUBLANES, LANES), F32)
        for s in range(SUBLANES):
            d = jnp.where(sub >= s, b3 - b3[:, s:s + 1, :], NEG_BIG)
            e = q3 * jnp.exp(d) * k3[:, s:s + 1, :]
            o3 = o3 + jnp.sum(e, axis=-1, keepdims=True) * v3[:, s:s + 1, :]
        finish(h, acc_ref[slot] + o3.reshape(C, LANES))

    per_iter = acc_ref.shape[0]
    assert hh % per_iter == 0

    def heads(i, c):
        hs = [i * per_iter + j for j in range(per_iter)]
        spreads = common(hs)
        for slot, (h, spread) in enumerate(zip(hs, spreads)):
            @pl.when(jnp.logical_not(spread <= GLA_MAX_SPREAD))
            def _(h=h, slot=slot):
                exact(h, slot)
        return c

    lax.fori_loop(0, hh // per_iter, heads, 0)


def _gla(q, k, lf, v, sg, gn, nseq, slen, s0):
    H, T, _ = q.shape
    C = _tile(slen, 256)
    tiles = slen // C
    n_levels = _log2(C // min(GLA_BLOCK, C))
    hh = _tile(H * LANES, 16 * LANES) // LANES
    per_iter = math.gcd(hh, GLA_HEADS_PER_ITER)
    row_spec = pl.BlockSpec((hh, C, LANES), lambda s, g, t: (g, s * tiles + t, 0))
    st_spec = pl.BlockSpec((1, hh, LANES, LANES), lambda s, g, t: (s, g, 0, 0))
    in_specs = [row_spec] * 5 + [pl.BlockSpec((hh, 1, LANES), lambda s, g, t: (g, 0, 0))]
    args = [q, k, lf, v, sg, gn]
    if s0 is not None:
        in_specs.append(st_spec)
        args.append(s0)
    return pl.pallas_call(
        functools.partial(_gla_kernel, C=C, hh=hh, has_init=s0 is not None, one_tile=tiles == 1),
        grid=(nseq, H // hh, tiles),
        in_specs=in_specs,
        out_specs=[row_spec, st_spec],
        out_shape=[jax.ShapeDtypeStruct((H, T, LANES), BF16),
                   jax.ShapeDtypeStruct((nseq, H, LANES, LANES), F32)],
        scratch_shapes=[pltpu.VMEM((hh, LANES, LANES), F32),
                        pltpu.VMEM((per_iter, C, LANES), F32),
                        pltpu.VMEM((per_iter, 4 + 2 * n_levels, C, LANES), BF16)],
        compiler_params=_cparams("parallel", "parallel", "arbitrary"),
        name="gla_init" if s0 is not None else "gla",
    )(*args)


PREP_ROWS = 64
PREP_BLOCK = 256


def _cast_pad_halves_kernel(w_ref, o_ref):
    n = w_ref.shape[1]
    o_ref[:, :n] = w_ref[...].astype(o_ref.dtype)
    if o_ref.shape[1] > n:
        o_ref[:, n:] = jnp.zeros((o_ref.shape[0], o_ref.shape[1] - n), o_ref.dtype)


def _cast_pad_halves(w, layer, pad):
    _, D, F2 = w.shape
    F = F2 // 2
    tr = _tile(D, PREP_ROWS)
    return pl.pallas_call(
        _cast_pad_halves_kernel,
        grid=(D // tr, 2),
        in_specs=[pl.BlockSpec((None, tr, F), lambda r, h: (layer, r, h))],
        out_specs=pl.BlockSpec((tr, F + pad), lambda r, h: (r, h)),
        out_shape=jax.ShapeDtypeStruct((D, 2 * (F + pad)), BF16),
        compiler_params=_cparams("parallel", "parallel"),
        name="cast_pad_halves",
    )(w)


def _cast_pad_rows_kernel(w_ref, o_ref, *, nb):
    r = pl.program_id(0)
    o_ref[...] = jnp.where(r < nb, w_ref[...], 0.0).astype(o_ref.dtype)


def _cast_pad_rows(w, layer, pad):
    _, F, D = w.shape
    rb = PREP_BLOCK
    assert F % rb == 0 and pad % rb == 0
    nb = F // rb
    return pl.pallas_call(
        functools.partial(_cast_pad_rows_kernel, nb=nb),
        grid=((F + pad) // rb,),
        in_specs=[pl.BlockSpec((None, rb, D), lambda r: (layer, jnp.minimum(r, nb - 1), 0))],
        out_specs=pl.BlockSpec((rb, D), lambda r: (r, 0)),
        out_shape=jax.ShapeDtypeStruct((F + pad, D), BF16),
        compiler_params=_cparams("parallel"),
        name="cast_pad_rows",
    )(w)


FF_ALIGN = 512


def kernel(x_prompt, x_sample, state_hgrn, cache_conv, a_w_in, a_ln_g, a_ln_b, a_w_s, a_b_s, a_w_out,
           b_w_in, b_lb, b_gn_g, b_w_out, f_w_up, f_conv_w, f_conv_b, f_w_down, ln1_g, ln1_b, ln2_g, ln2_b):
    depth = ln1_g.shape[0]
    alpha = (2 * depth) ** 0.25
    bp, lp, D = x_prompt.shape
    bs, ls, _ = x_sample.shape
    blk = a_w_s.shape[-1]
    assert blk == LANES and lp % blk == 0 and blk % ls == 0 and (bs * ls) % blk == 0
    d_ff = f_conv_w.shape[-1]
    f_pad = -d_ff % FF_ALIGN
    H = b_gn_g.shape[-1] // LANES

    xp = x_prompt.reshape(bp * lp, D)
    xs = x_sample.reshape(bs * ls, D)
    groups = [[xp, xp.astype(BF16), bp, lp], [xs, xs.astype(BF16), bs, ls]]

    v_rows, s_prompt, s_sample, c_prompt, c_sample = [], [], [], [], []
    for i in range(depth):
        j = i // 2
        row = lambda a: a[i][None, :]
        if i % 2 == 0:
            w_in = a_w_in[j].astype(BF16)
            w_out = a_w_out[j].astype(BF16)
            ln_g, ln_b = a_ln_g[j][None, :], a_ln_b[j][None, :]
            reps = blk // ls
            ws_variants = [a_w_s[j], jnp.tile(a_w_s[j][:, :ls, :ls], (1, reps, reps))]
            bs_variants = [a_b_s[j].T, jnp.tile(a_b_s[j][:, :ls], (1, reps)).T]
            for gi, grp in enumerate(groups):
                xf, xb, nseq, slen = grp
                u, v = _a_in(xb, w_in)
                gated, vn = _spatial(u, v, ln_g, ln_b, ws_variants[gi], bs_variants[gi],
                                     period=min(slen, blk), want_vn=gi == 1)
                if gi == 1:
                    v_rows.append(vn.reshape(nseq, slen, -1))
                grp[0], grp[1] = _mm_res_ln(gated, w_out, xf, row(ln1_g), row(ln1_b), alpha, head_major=False)
        else:
            w_in = b_w_in[j].astype(BF16)
            w_out = b_w_out[j].astype(BF16)
            gn = b_gn_g[j].reshape(H, 1, LANES)
            for gi, grp in enumerate(groups):
                xf, xb, nseq, slen = grp
                q, k, lf, v, sg = _hgrn_in(xb, w_in, b_lb, i)
                o, s_fin = _gla(q, k, lf, v, sg, gn, nseq, slen, state_hgrn[j] if gi == 1 else None)
                (s_sample if gi == 1 else s_prompt).append(s_fin)
                grp[0], grp[1] = _mm_res_ln(o, w_out, xf, row(ln1_g), row(ln1_b), alpha, head_major=True)

        pad_cols = lambda a: jnp.pad(a, ((0, 0), (0, f_pad)))
        w_up = _cast_pad_halves(f_w_up, i, f_pad)
        cw = pad_cols(f_conv_w[i])
        cb = pad_cols(f_conv_b[i][None, :])
        wd = _cast_pad_rows(f_w_down, i, f_pad)
        for gi, grp in enumerate(groups):
            xf, xb, nseq, slen = grp
            state = None
            if gi == 1:
                state = jnp.pad(cache_conv[i], ((0, 0), (SUBLANES - 2, 0), (0, f_pad)))
            gated, tail = _ffn_up(xb, w_up, cw, cb, nseq, slen, state)
            (c_sample if gi == 1 else c_prompt).append(tail[:, SUBLANES - 2:, :d_ff])
            grp[0], grp[1] = _mm_res_ln(gated, wd, xf, row(ln2_g), row(ln2_b), alpha, head_major=False)

    return (groups[0][0].reshape(bp, lp, D), groups[1][0].reshape(bs, ls, D), jnp.stack(v_rows),
            jnp.stack(s_prompt), jnp.stack(s_sample), jnp.stack(c_prompt), jnp.stack(c_sample))
```

```python
import functools
import math

import jax
import jax.numpy as jnp
from jax import lax
from jax.experimental import pallas as pl
from jax.experimental.pallas import tpu as pltpu

F32 = jnp.float32
BF16 = jnp.bfloat16

LN_EPS = 1e-5
RMS_EPS = 1e-6
LANES = 128
SUBLANES = 8
VMEM_LIMIT_BYTES = 56 * 1024 * 1024
NEG_BIG = -1e30


def _cparams(*sem):
    return pltpu.CompilerParams(dimension_semantics=sem, vmem_limit_bytes=VMEM_LIMIT_BYTES)


def _tile(n, pref):
    t = min(n, pref)
    while n % t:
        t -= LANES
    return t


def _dot(a, b):
    return jnp.dot(a, b, preferred_element_type=F32)


def _log2(n):
    assert n > 0 and n & (n - 1) == 0, f"{n} is not a power of two"
    return n.bit_length() - 1


def _block_id(idx, size):
    return lax.shift_right_logical(idx, _log2(size))


def _layer_norm_rows(x, g, b):
    mu = jnp.mean(x, axis=-1, keepdims=True)
    xc = x - mu
    var = jnp.mean(xc * xc, axis=-1, keepdims=True)
    return xc * lax.rsqrt(var + LN_EPS) * g + b


A_SUB_COLS = 256


def _a_in_kernel(x_ref, wu_ref, wv_ref, u_ref, v_ref):
    x = x_ref[...]
    sub = _tile(u_ref.shape[1], A_SUB_COLS)
    for w_ref, o_ref in ((wu_ref, u_ref), (wv_ref, v_ref)):
        for j in range(u_ref.shape[1] // sub):
            cs = slice(j * sub, (j + 1) * sub)
            o_ref[:, cs] = jax.nn.gelu(_dot(x, w_ref[:, cs])).astype(o_ref.dtype)


def _a_in(xb, w_in):
    T, D = xb.shape
    W = w_in.shape[1] // 2
    tm, tn = _tile(T, 1024), _tile(W, 512)
    nn = W // tn
    return pl.pallas_call(
        _a_in_kernel,
        grid=(T // tm, nn),
        in_specs=[pl.BlockSpec((tm, D), lambda m, n: (m, 0)),
                  pl.BlockSpec((D, tn), lambda m, n: (0, n)),
                  pl.BlockSpec((D, tn), lambda m, n: (0, n + nn))],
        out_specs=[pl.BlockSpec((tm, tn), lambda m, n: (m, n)),
                   pl.BlockSpec((tm, tn), lambda m, n: (m, n))],
        out_shape=[jax.ShapeDtypeStruct((T, W), BF16), jax.ShapeDtypeStruct((T, W), BF16)],
        compiler_params=_cparams("parallel", "arbitrary"),
        name="a_in",
    )(xb, w_in, w_in)


def _spatial_kernel(u_ref, v_ref, g_ref, b_ref, ws_ref, bs_ref, o_ref, vn_ref, *, nb, groups, period):
    vn_ref[...] = _layer_norm_rows(v_ref[...].astype(F32), g_ref[...], b_ref[...])
    row = lax.broadcasted_iota(jnp.int32, (LANES, LANES), 0)
    col = lax.broadcasted_iota(jnp.int32, (LANES, LANES), 1)
    keep = (col <= row) & (_block_id(row, period) == _block_id(col, period))
    for g in range(groups):
        cs = slice(g * LANES, (g + 1) * LANES)
        wsg = jnp.where(keep, ws_ref[g], 0.0).astype(BF16)
        bias = bs_ref[:, g:g + 1]
        for j in range(nb):
            rs = slice(j * LANES, (j + 1) * LANES)
            mixed = _dot(wsg, vn_ref[rs, cs].astype(BF16)) + bias
            o_ref[rs, cs] = (u_ref[rs, cs].astype(F32) * mixed).astype(o_ref.dtype)


def _spatial(u, v, ln_g, ln_b, ws, bs_t, period, want_vn):
    T, W = v.shape
    G = ws.shape[0]
    tb = _tile(T, 2 * LANES)
    row_spec = pl.BlockSpec((tb, W), lambda i: (i, 0))
    vec_spec = pl.BlockSpec((1, W), lambda i: (0, 0))
    kern = functools.partial(_spatial_kernel, nb=tb // LANES, groups=G, period=period)
    common = dict(
        grid=(T // tb,),
        in_specs=[row_spec, row_spec, vec_spec, vec_spec,
                  pl.BlockSpec((G, LANES, LANES), lambda i: (0, 0, 0)),
                  pl.BlockSpec((LANES, G), lambda i: (0, 0))],
        compiler_params=_cparams("parallel"),
    )
    if want_vn:
        return pl.pallas_call(
            kern, out_specs=[row_spec, row_spec],
            out_shape=[jax.ShapeDtypeStruct((T, W), BF16), jax.ShapeDtypeStruct((T, W), F32)],
            name="spatial_vn", **common)(u, v, ln_g, ln_b, ws, bs_t)
    gated = pl.pallas_call(
        kern, out_specs=row_spec, out_shape=jax.ShapeDtypeStruct((T, W), BF16),
        scratch_shapes=[pltpu.VMEM((tb, W), F32)],
        name="spatial", **common)(u, v, ln_g, ln_b, ws, bs_t)
    return gated, None


LN_ROWS = 64
LN_COLS = 1024
MM_K_TILE = 2816


def _ln_inplace(acc_ref, g_ref, b_ref, ob_ref, after_chunk):
    rows, n = acc_ref.shape
    cw = _tile(n, LN_COLS)
    slabs = [slice(j * cw, (j + 1) * cw) for j in range(n // cw)]

    def body(i, c):
        rs = pl.ds(pl.multiple_of(i * LN_ROWS, LN_ROWS), LN_ROWS)
        mu = sum(jnp.sum(acc_ref[rs, cs], axis=-1, keepdims=True) for cs in slabs) * (1.0 / n)
        var = sum(jnp.sum(jnp.square(acc_ref[rs, cs] - mu), axis=-1, keepdims=True) for cs in slabs) * (1.0 / n)
        r = lax.rsqrt(var + LN_EPS)
        for cs in slabs:
            y = (acc_ref[rs, cs] - mu) * r * g_ref[:, cs] + b_ref[:, cs]
            acc_ref[rs, cs] = y
            ob_ref[rs, cs] = y.astype(ob_ref.dtype)
        after_chunk(i)
        return c

    lax.fori_loop(0, rows // LN_ROWS, body, 0)


def _mm_res_ln_kernel(x_ref, w_ref, res_ref, g_ref, b_ref, of_hbm, ob_hbm, acc_ref, ob_ref, sem, *scratch,
                      alpha, nm, nk, nn, tn, heads):
    m = pl.program_id(0)
    k = pl.program_id(1)
    n = pl.program_id(2)
    tm = acc_ref.shape[0]
    n_chunks = tm // LN_ROWS

    def copies(tile, i):
        src = pl.ds(pl.multiple_of(i * LN_ROWS, LN_ROWS), LN_ROWS)
        dst = pl.ds(pl.multiple_of(tile * tm + i * LN_ROWS, LN_ROWS), LN_ROWS)
        return (pltpu.make_async_copy(acc_ref.at[src], of_hbm.at[dst], sem.at[0]),
                pltpu.make_async_copy(ob_ref.at[src], ob_hbm.at[dst], sem.at[1]))

    def start_chunk(i):
        for cp in copies(m, i):
            cp.start()

    def wait_tile(tile):
        def body(i, c):
            for cp in copies(tile, i):
                cp.wait()
            return c
        lax.fori_loop(0, n_chunks, body, 0)

    @pl.when((m > 0) & (k == 0) & (n == 0))
    def _():
        wait_tile(m - 1)

    if heads:
        xs_ref, = scratch

        @pl.when(n == 0)
        def _():
            for h in range(heads):
                xs_ref[:, h * LANES:(h + 1) * LANES] = x_ref[h]

        x = xs_ref[...]
    else:
        x = x_ref[...]
    cols = pl.ds(pl.multiple_of(n * tn, tn), tn)

    @pl.when(k == 0)
    def _():
        acc_ref[:, cols] = alpha * res_ref[...]

    acc_ref[:, cols] += _dot(x, w_ref[...])

    @pl.when((k == nk - 1) & (n == nn - 1))
    def _():
        _ln_inplace(acc_ref, g_ref, b_ref, ob_ref, start_chunk)

    @pl.when((m == nm - 1) & (k == nk - 1) & (n == nn - 1))
    def _():
        wait_tile(m)


def _mm_res_ln(x, w, res, ln_g, ln_b, alpha, head_major):
    T, N = res.shape
    K = w.shape[0]
    tm, tn = _tile(T, 1024), _tile(N, 512)
    tk = K if K <= 4096 else _tile(K, MM_K_TILE)
    nk, nn = K // tk, N // tn
    heads = K // LANES if head_major else 0
    once = dict(pipeline_mode=pl.Buffered(1))
    if head_major:
        assert nk == 1
        x_spec = pl.BlockSpec((heads, tm, LANES), lambda m, k, n: (0, m, 0), **once)
        scratch = [pltpu.VMEM((tm, K), BF16)]
    elif nk == 1:
        x_spec = pl.BlockSpec((tm, K), lambda m, k, n: (m, 0), **once)
        scratch = []
    else:
        x_spec = pl.BlockSpec((tm, tk), lambda m, k, n: (m, k))
        scratch = []
    vec_spec = pl.BlockSpec((1, N), lambda m, k, n: (0, 0))
    res_spec = pl.BlockSpec((tm, tn), lambda m, k, n: (m, jnp.where(k == 0, n, nn - 1)))
    hbm_spec = pl.BlockSpec(memory_space=pl.ANY)
    assert tm % LN_ROWS == 0
    return pl.pallas_call(
        functools.partial(_mm_res_ln_kernel, alpha=alpha, nm=T // tm, nk=nk, nn=nn, tn=tn, heads=heads),
        grid=(T // tm, nk, nn),
        in_specs=[x_spec,
                  pl.BlockSpec((tk, tn), lambda m, k, n: (k, n)),
                  res_spec, vec_spec, vec_spec],
        out_specs=[hbm_spec, hbm_spec],
        out_shape=[jax.ShapeDtypeStruct((T, N), F32), jax.ShapeDtypeStruct((T, N), BF16)],
        scratch_shapes=[pltpu.VMEM((tm, N), F32), pltpu.VMEM((tm, N), BF16),
                        pltpu.SemaphoreType.DMA((2,))] + scratch,
        compiler_params=_cparams("arbitrary", "arbitrary", "arbitrary"),
        name="mm_res_ln",
    )(x, w, res, ln_g, ln_b)


FFN_SUB_COLS = 256


def _ffn_up_kernel(x_ref, wa_ref, wu_ref, cw_ref, cb_ref, *rest, nseq, slen, tiles_per_seq, carry):
    if carry:
        g_ref, cs_ref, halo_ref = rest
    else:
        st_ref, g_ref, cs_ref = rest
    m = pl.program_id(0)
    f = pl.program_id(1)
    tm, tf = g_ref.shape
    sub = _tile(tf, FFN_SUB_COLS)

    if carry:
        @pl.when(m % tiles_per_seq == 0)
        def _():
            halo_ref[f] = jnp.zeros((SUBLANES, tf), F32)

    _log2(slen)
    pos = lax.broadcasted_iota(jnp.int32, (tm, 1), 0) & (slen - 1)
    x = x_ref[...]
    acts = []
    for j in range(tf // sub):
        cs = slice(j * sub, (j + 1) * sub)
        a = _dot(x, wa_ref[:, cs])
        if carry:
            st = halo_ref[f, :, cs][None]
        else:
            st = st_ref[:, :, cs]

        def expand(rows):
            return jnp.broadcast_to(rows, (nseq, slen, sub)).reshape(tm, sub)

        s_m1 = expand(st[:, SUBLANES - 1:SUBLANES, :])
        s_m2 = expand(st[:, SUBLANES - 2:SUBLANES - 1, :])
        prev1 = jnp.where(pos == 0, s_m1, pltpu.roll(a, 1, 0))
        prev2 = jnp.where(pos == 0, s_m2, jnp.where(pos == 1, s_m1, pltpu.roll(a, 2, 0)))
        cw = cw_ref[:, cs]
        c = cb_ref[:, cs] + prev2 * cw[0:1, :] + prev1 * cw[1:2, :] + a * cw[2:3, :]

        tail = a.reshape(nseq, slen, sub)[:, slen - SUBLANES:, :]
        cs_ref[:, :, cs] = tail
        if carry:
            halo_ref[f, :, cs] = tail[0]
        acts.append(jax.nn.silu(c))
    for j, act in enumerate(acts):
        cs = slice(j * sub, (j + 1) * sub)
        g_ref[:, cs] = (act * _dot(x, wu_ref[:, cs])).astype(g_ref.dtype)


def _ffn_up(xb, w_up, cw, cb, nseq, slen, state):
    T, D = xb.shape
    F = w_up.shape[1] // 2
    tf = _tile(F, 512)
    nf = F // tf
    carry = state is None
    if carry:
        tm = _tile(slen, 1024)
        tiles_per_seq = slen // tm
        seq_per_tile = 1
        cs_spec = pl.BlockSpec((1, SUBLANES, tf), lambda m, f: (m, 0, f))
        extra_in, extra_specs = [], []
        scratch = [pltpu.VMEM((nf, SUBLANES, tf), F32)]
    else:
        tm = _tile(T, 1024)
        assert tm % slen == 0
        tiles_per_seq = 1
        seq_per_tile = tm // slen
        cs_spec = pl.BlockSpec((seq_per_tile, SUBLANES, tf), lambda m, f: (m, 0, f))
        extra_in, extra_specs = [state], [cs_spec]
        scratch = []
    kern = functools.partial(_ffn_up_kernel, nseq=seq_per_tile, slen=tm // seq_per_tile,
                             tiles_per_seq=tiles_per_seq, carry=carry)
    n_tails = (T // tm) * seq_per_tile
    gated, tails = pl.pallas_call(
        kern,
        grid=(T // tm, nf),
        in_specs=[pl.BlockSpec((tm, D), lambda m, f: (m, 0)),
                  pl.BlockSpec((D, tf), lambda m, f: (0, f)),
                  pl.BlockSpec((D, tf), lambda m, f: (0, f + nf)),
                  pl.BlockSpec((3, tf), lambda m, f: (0, f)),
                  pl.BlockSpec((1, tf), lambda m, f: (0, f))] + extra_specs,
        out_specs=[pl.BlockSpec((tm, tf), lambda m, f: (m, f)), cs_spec],
        out_shape=[jax.ShapeDtypeStruct((T, F), BF16), jax.ShapeDtypeStruct((n_tails, SUBLANES, F), F32)],
        scratch_shapes=scratch,
        compiler_params=_cparams("arbitrary", "arbitrary"),
        name="ffn_up_carry" if carry else "ffn_up_state",
    )(xb, w_up, w_up, cw, cb, *extra_in)
    return gated, tails[tiles_per_seq - 1::tiles_per_seq]


def _hgrn_in_kernel(x_ref, wq_ref, wf_ref, wi_ref, wg_ref, lb_ref,
                    q_ref, k_ref, lf_ref, v_ref, sg_ref, *, layer, hh):
    x = x_ref[...]
    lbv = lb_ref[...]
    e = jnp.exp(lbv - jnp.max(lbv, axis=0, keepdims=True))
    p = e / jnp.sum(e, axis=0, keepdims=True)
    lb = jnp.sum(p[1:layer + 1, :], axis=0, keepdims=True)

    def put(ref, val):
        for j in range(hh):
            ref[j] = val[:, j * LANES:(j + 1) * LANES].astype(ref.dtype)

    put(q_ref, _dot(x, wq_ref[...]))
    f = lb + (1.0 - lb) * jax.nn.sigmoid(_dot(x, wf_ref[...]))
    put(lf_ref, jnp.log(f))
    put(k_ref, 1.0 - f)
    put(v_ref, _dot(x, wi_ref[...]))
    put(sg_ref, jax.nn.silu(_dot(x, wg_ref[...])))


def _hgrn_in(xb, w_in, b_lb, layer):
    T, D = xb.shape
    QK = w_in.shape[1] // 4
    H = QK // LANES
    tm, tn = _tile(T, 1024), _tile(QK, 256)
    nn = QK // tn
    hh = tn // LANES
    depth = b_lb.shape[0]
    w_specs = [pl.BlockSpec((D, tn), lambda m, n, k=k: (0, n + k * nn)) for k in range(4)]
    o_spec = pl.BlockSpec((hh, tm, LANES), lambda m, n: (n, m, 0))
    hm = lambda dt: jax.ShapeDtypeStruct((H, T, LANES), dt)
    return pl.pallas_call(
        functools.partial(_hgrn_in_kernel, layer=layer, hh=hh),
        grid=(T // tm, nn),
        in_specs=[pl.BlockSpec((tm, D), lambda m, n: (m, 0))] + w_specs
                 + [pl.BlockSpec((depth, tn), lambda m, n: (0, n))],
        out_specs=[o_spec] * 5,
        out_shape=[hm(BF16), hm(BF16), hm(F32), hm(BF16), hm(BF16)],
        compiler_params=_cparams("parallel", "arbitrary"),
        name="hgrn_in",
    )(xb, w_in, w_in, w_in, w_in, b_lb)


GLA_BLOCK = 32
GLA_MAX_SPREAD = 60.0
GLA_HEADS_PER_ITER = 8

def _nt_dot(a, b):
    return lax.dot_general(a, b, (((1,), (1,)), ((), ())), preferred_element_type=F32)


def _tn_dot(a, b):
    return lax.dot_general(a, b, (((0,), (0,)), ((), ())), preferred_element_type=F32)


def _gla_kernel(q_ref, k_ref, lf_ref, v_ref, sg_ref, gn_ref, *rest, C, hh, has_init, one_tile):
    if has_init:
        s0_ref, o_ref, sout_ref, st_ref, acc_ref, ops_ref = rest
    else:
        o_ref, sout_ref, st_ref, acc_ref, ops_ref = rest
    t = pl.program_id(2)
    direct_state = has_init and one_tile

    if not direct_state:
        @pl.when(t == 0)
        def _():
            if has_init:
                def init(h, c):
                    st_ref[h] = s0_ref[0, h].T
                    return c
                lax.fori_loop(0, hh, init, 0)
            else:
                st_ref[...] = jnp.zeros(st_ref.shape, F32)

    row = lax.broadcasted_iota(jnp.int32, (C, C), 0)
    col = lax.broadcasted_iota(jnp.int32, (C, C), 1)
    tri = jnp.where(col <= row, 1.0, 0.0).astype(BF16)
    n8 = C // SUBLANES
    sub = lax.broadcasted_iota(jnp.int32, (1, SUBLANES, 1), 1)
    half = C // 2
    hrow = lax.broadcasted_iota(jnp.int32, (half, half), 0)
    hcol = lax.broadcasted_iota(jnp.int32, (half, half), 1)
    w0 = min(GLA_BLOCK, C)
    nb0 = C // w0
    tri0 = tri[:w0, :w0]
    dr = min(C, LANES)
    drow = lax.broadcasted_iota(jnp.int32, (dr, dr), 0)
    dcol = lax.broadcasted_iota(jnp.int32, (dr, dr), 1)

    def widths(lo, hi):
        out = []
        while lo < hi:
            out.append(lo)
            lo *= 2
        return out

    def halves(x, w, which):
        return jnp.concatenate([x[(2 * p + which) * w:(2 * p + which + 1) * w] for p in range(C // (2 * w))], axis=0)

    def rows_at(x, first, step, reps):
        return jnp.concatenate(
            [jnp.broadcast_to(x[i:i + 1, :], (reps, LANES)) for i in range(first, C, step)], axis=0)

    def load(h):
        q = q_ref[h].astype(F32)
        k = k_ref[h].astype(F32)
        vb = v_ref[h]
        g = lf_ref[h]
        g1 = g.astype(BF16)
        r1 = g - g1.astype(F32)
        g2 = r1.astype(BF16)
        g3 = (r1 - g2.astype(F32)).astype(BF16)
        b = _dot(tri, g1) + _dot(tri, g2) + _dot(tri, g3)
        return q, k, vb, vb.astype(F32), b

    def level(w, q, k, v32, b):
        bref = rows_at(b, w - 1, 2 * w, w)
        qt = (halves(q, w, 1) * jnp.exp(halves(b, w, 1) - bref)).astype(BF16)
        ks = (halves(k, w, 0) * jnp.exp(bref - halves(b, w, 0))).astype(BF16)
        att = _nt_dot(qt, ks)
        if C > 2 * w:
            att = jnp.where(_block_id(hrow, w) == _block_id(hcol, w), att, 0.0)
        return _dot(att.astype(BF16), halves(v32, w, 0).astype(BF16))

    def finish(h, o):
        o = o * lax.rsqrt(jnp.mean(o * o, axis=-1, keepdims=True) + RMS_EPS)
        o_ref[h] = (o * gn_ref[h] * sg_ref[h].astype(F32)).astype(o_ref.dtype)

    levels = widths(w0, C)
    Q_STATE, K_STATE, Q_BLOCK, K_BLOCK = 0, 1, 2, 3
    q_level = lambda l: 4 + 2 * l
    k_level = lambda l: 5 + 2 * l

    def prepare(h, slot):
        g = lf_ref[h]
        gw = jnp.concatenate([g[i * w0:(i + 1) * w0] for i in range(nb0)], axis=1)
        g1 = gw.astype(BF16)
        r1 = gw - g1.astype(F32)
        g2 = r1.astype(BF16)
        g3 = (r1 - g2.astype(F32)).astype(BF16)
        cw = _dot(tri0, g1) + _dot(tri0, g2) + _dot(tri0, g3)
        cbs = [cw[:, i * LANES:(i + 1) * LANES] for i in range(nb0)]
        b_first, b_last = [], []
        offset = None
        for cb in cbs:
            first, total = cb[0:1, :], cb[w0 - 1:w0, :]
            b_first.append(first if offset is None else offset + first)
            offset = total if offset is None else offset + total
            b_last.append(offset)
        b_end = offset
        spread = None
        for i, cb in enumerate(cbs):
            rs = slice(i * w0, (i + 1) * w0)
            first, total = cb[0:1, :], cb[w0 - 1:w0, :]
            bf, bl = b_first[i], b_last[i]
            qp = q_ref[h, rs, :].astype(F32) * jnp.exp(cb - first)
            kc = k_ref[h, rs, :].astype(F32)
            kq = kc * jnp.exp(total - cb)
            ops_ref[slot, Q_STATE, rs, :] = (qp * jnp.exp(bf)).astype(BF16)
            ops_ref[slot, K_STATE, rs, :] = (kq * jnp.exp(b_end - bl)).astype(BF16)
            ops_ref[slot, Q_BLOCK, rs, :] = qp.astype(BF16)
            ops_ref[slot, K_BLOCK, rs, :] = (kc * jnp.exp(jnp.minimum(first - cb, GLA_MAX_SPREAD))).astype(BF16)
            for l, w in enumerate(levels):
                p, off = divmod(i * w0, 2 * w)
                br = b_last[(p * 2 * w + w - 1) // w0]
                if off >= w:
                    dst = slice(p * w + off - w, p * w + off - w + w0)
                    ops_ref[slot, q_level(l), dst, :] = (qp * jnp.exp(bf - br)).astype(BF16)
                else:
                    dst = slice(p * w + off, p * w + off + w0)
                    ops_ref[slot, k_level(l), dst, :] = (kq * jnp.exp(br - bl)).astype(BF16)
            spread = first - total if spread is None else jnp.maximum(spread, first - total)
        return jnp.max(spread), b_end

    def attend(h, slot, b_end):
        vb = v_ref[h]
        st = s0_ref[0, h].T if direct_state else st_ref[h]
        o_state = _nt_dot(ops_ref[slot, Q_STATE], st.astype(BF16))
        acc_ref[slot] = o_state
        st_new = st * jnp.exp(b_end) + _tn_dot(vb, ops_ref[slot, K_STATE])
        if not direct_state:
            st_ref[h] = st_new
        sout_ref[0, h] = st_new.T
        blk = [o_state[i * w0:(i + 1) * w0] for i in range(nb0)]
        for l, w in enumerate(levels):
            att = _nt_dot(ops_ref[slot, q_level(l), 0:half, :], ops_ref[slot, k_level(l), 0:half, :])
            if C > 2 * w:
                att = jnp.where(_block_id(hrow, w) == _block_id(hcol, w), att, 0.0)
            ot = _dot(att.astype(BF16), halves(vb, w, 0))
            for p in range(C // (2 * w)):
                for i in range(w // w0):
                    blk[(2 * p + 1) * w // w0 + i] += ot[p * w + i * w0:p * w + (i + 1) * w0]
        keep = (_block_id(drow, w0) == _block_id(dcol, w0)) & (dcol <= drow)
        for r in range(C // dr):
            rs = slice(r * dr, (r + 1) * dr)
            att = jnp.where(keep, _nt_dot(ops_ref[slot, Q_BLOCK, rs, :], ops_ref[slot, K_BLOCK, rs, :]), 0.0)
            od = _dot(att.astype(BF16), vb[rs])
            for i in range(dr // w0):
                blk[r * dr // w0 + i] += od[i * w0:(i + 1) * w0]
        finish(h, jnp.concatenate(blk, axis=0))

    def common(hs):
        prepared = [prepare(h, slot) for slot, h in enumerate(hs)]
        for slot, (h, (_, b_end)) in enumerate(zip(hs, prepared)):
            attend(h, slot, b_end)
        return [spread for spread, _ in prepared]

    def exact(h, slot):
        q, k, _, v32, b = load(h)
        for w in widths(SUBLANES, C):
            ot = level(w, q, k, v32, b)
            for p in range(C // (2 * w)):
                acc_ref[slot, (2 * p + 1) * w:(2 * p + 2) * w, :] += ot[p * w:(p + 1) * w, :]
        b3 = b.reshape(n8, SUBLANES, LANES)
        q3 = q.reshape(n8, SUBLANES, LANES)
        k3 = k.reshape(n8, SUBLANES, LANES)
        v3 = v32.reshape(n8, SUBLANES, LANES)
        o3 = jnp.zeros((n8, SUBLANES, LANES), F32)
        for s in range(SUBLANES):
            d = jnp.where(sub >= s, b3 - b3[:, s:s + 1, :], NEG_BIG)
            e = q3 * jnp.exp(d) * k3[:, s:s + 1, :]
            o3 = o3 + jnp.sum(e, axis=-1, keepdims=True) * v3[:, s:s + 1, :]
        finish(h, acc_ref[slot] + o3.reshape(C, LANES))

    per_iter = acc_ref.shape[0]
    assert hh % per_iter == 0

    def heads(i, c):
        hs = [i * per_iter + j for j in range(per_iter)]
        spreads = common(hs)
        for slot, (h, spread) in enumerate(zip(hs, spreads)):
            @pl.when(jnp.logical_not(spread <= GLA_MAX_SPREAD))
            def _(h=h, slot=slot):
                exact(h, slot)
        return c

    lax.fori_loop(0, hh // per_iter, heads, 0)


def _gla(q, k, lf, v, sg, gn, nseq, slen, s0):
    H, T, _ = q.shape
    C = _tile(slen, 256)
    tiles = slen // C
    n_levels = _log2(C // min(GLA_BLOCK, C))
    hh = _tile(H * LANES, 16 * LANES) // LANES
    per_iter = math.gcd(hh, GLA_HEADS_PER_ITER)
    row_spec = pl.BlockSpec((hh, C, LANES), lambda s, g, t: (g, s * tiles + t, 0))
    st_spec = pl.BlockSpec((1, hh, LANES, LANES), lambda s, g, t: (s, g, 0, 0))
    in_specs = [row_spec] * 5 + [pl.BlockSpec((hh, 1, LANES), lambda s, g, t: (g, 0, 0))]
    args = [q, k, lf, v, sg, gn]
    if s0 is not None:
        in_specs.append(st_spec)
        args.append(s0)
    return pl.pallas_call(
        functools.partial(_gla_kernel, C=C, hh=hh, has_init=s0 is not None, one_tile=tiles == 1),
        grid=(nseq, H // hh, tiles),
        in_specs=in_specs,
        out_specs=[row_spec, st_spec],
        out_shape=[jax.ShapeDtypeStruct((H, T, LANES), BF16),
                   jax.ShapeDtypeStruct((nseq, H, LANES, LANES), F32)],
        scratch_shapes=[pltpu.VMEM((hh, LANES, LANES), F32),
                        pltpu.VMEM((per_iter, C, LANES), F32),
                        pltpu.VMEM((per_iter, 4 + 2 * n_levels, C, LANES), BF16)],
        compiler_params=_cparams("parallel", "parallel", "arbitrary"),
        name="gla_init" if s0 is not None else "gla",
    )(*args)


PREP_ROWS = 64
PREP_BLOCK = 256


def _cast_pad_halves_kernel(w_ref, o_ref):
    n = w_ref.shape[1]
    o_ref[:, :n] = w_ref[...].astype(o_ref.dtype)
    if o_ref.shape[1] > n:
        o_ref[:, n:] = jnp.zeros((o_ref.shape[0], o_ref.shape[1] - n), o_ref.dtype)


def _cast_pad_halves(w, layer, pad):
    _, D, F2 = w.shape
    F = F2 // 2
    tr = _tile(D, PREP_ROWS)
    return pl.pallas_call(
        _cast_pad_halves_kernel,
        grid=(D // tr, 2),
        in_specs=[pl.BlockSpec((None, tr, F), lambda r, h: (layer, r, h))],
        out_specs=pl.BlockSpec((tr, F + pad), lambda r, h: (r, h)),
        out_shape=jax.ShapeDtypeStruct((D, 2 * (F + pad)), BF16),
        compiler_params=_cparams("parallel", "parallel"),
        name="cast_pad_halves",
    )(w)


def _cast_pad_rows_kernel(w_ref, o_ref, *, nb):
    r = pl.program_id(0)
    o_ref[...] = jnp.where(r < nb, w_ref[...], 0.0).astype(o_ref.dtype)


def _cast_pad_rows(w, layer, pad):
    _, F, D = w.shape
    rb = PREP_BLOCK
    assert F % rb == 0 and pad % rb == 0
    nb = F // rb
    return pl.pallas_call(
        functools.partial(_cast_pad_rows_kernel, nb=nb),
        grid=((F + pad) // rb,),
        in_specs=[pl.BlockSpec((None, rb, D), lambda r: (layer, jnp.minimum(r, nb - 1), 0))],
        out_specs=pl.BlockSpec((rb, D), lambda r: (r, 0)),
        out_shape=jax.ShapeDtypeStruct((F + pad, D), BF16),
        compiler_params=_cparams("parallel"),
        name="cast_pad_rows",
    )(w)


FF_ALIGN = 512


def kernel(x_prompt, x_sample, state_hgrn, cache_conv, a_w_in, a_ln_g, a_ln_b, a_w_s, a_b_s, a_w_out,
           b_w_in, b_lb, b_gn_g, b_w_out, f_w_up, f_conv_w, f_conv_b, f_w_down, ln1_g, ln1_b, ln2_g, ln2_b):
    depth = ln1_g.shape[0]
    alpha = (2 * depth) ** 0.25
    bp, lp, D = x_prompt.shape
    bs, ls, _ = x_sample.shape
    blk = a_w_s.shape[-1]
    assert blk == LANES and lp % blk == 0 and blk % ls == 0 and (bs * ls) % blk == 0
    d_ff = f_conv_w.shape[-1]
    f_pad = -d_ff % FF_ALIGN
    H = b_gn_g.shape[-1] // LANES

    xp = x_prompt.reshape(bp * lp, D)
    xs = x_sample.reshape(bs * ls, D)
    groups = [[xp, xp.astype(BF16), bp, lp], [xs, xs.astype(BF16), bs, ls]]

    v_rows, s_prompt, s_sample, c_prompt, c_sample = [], [], [], [], []
    for i in range(depth):
        j = i // 2
        row = lambda a: a[i][None, :]
        if i % 2 == 0:
            w_in = a_w_in[j].astype(BF16)
            w_out = a_w_out[j].astype(BF16)
            ln_g, ln_b = a_ln_g[j][None, :], a_ln_b[j][None, :]
            reps = blk // ls
            ws_variants = [a_w_s[j], jnp.tile(a_w_s[j][:, :ls, :ls], (1, reps, reps))]
            bs_variants = [a_b_s[j].T, jnp.tile(a_b_s[j][:, :ls], (1, reps)).T]
            for gi, grp in enumerate(groups):
                xf, xb, nseq, slen = grp
                u, v = _a_in(xb, w_in)
                gated, vn = _spatial(u, v, ln_g, ln_b, ws_variants[gi], bs_variants[gi],
                                     period=min(slen, blk), want_vn=gi == 1)
                if gi == 1:
                    v_rows.append(vn.reshape(nseq, slen, -1))
                grp[0], grp[1] = _mm_res_ln(gated, w_out, xf, row(ln1_g), row(ln1_b), alpha, head_major=False)
        else:
            w_in = b_w_in[j].astype(BF16)
            w_out = b_w_out[j].astype(BF16)
            gn = b_gn_g[j].reshape(H, 1, LANES)
            for gi, grp in enumerate(groups):
                xf, xb, nseq, slen = grp
                q, k, lf, v, sg = _hgrn_in(xb, w_in, b_lb, i)
                o, s_fin = _gla(q, k, lf, v, sg, gn, nseq, slen, state_hgrn[j] if gi == 1 else None)
                (s_sample if gi == 1 else s_prompt).append(s_fin)
                grp[0], grp[1] = _mm_res_ln(o, w_out, xf, row(ln1_g), row(ln1_b), alpha, head_major=True)

        pad_cols = lambda a: jnp.pad(a, ((0, 0), (0, f_pad)))
        w_up = _cast_pad_halves(f_w_up, i, f_pad)
        cw = pad_cols(f_conv_w[i])
        cb = pad_cols(f_conv_b[i][None, :])
        wd = _cast_pad_rows(f_w_down, i, f_pad)
        for gi, grp in enumerate(groups):
            xf, xb, nseq, slen = grp
            state = None
            if gi == 1:
                state = jnp.pad(cache_conv[i], ((0, 0), (SUBLANES - 2, 0), (0, f_pad)))
            gated, tail = _ffn_up(xb, w_up, cw, cb, nseq, slen, state)
            (c_sample if gi == 1 else c_prompt).append(tail[:, SUBLANES - 2:, :d_ff])
            grp[0], grp[1] = _mm_res_ln(gated, wd, xf, row(ln2_g), row(ln2_b), alpha, head_major=False)

    return (groups[0][0].reshape(bp, lp, D), groups[1][0].reshape(bs, ls, D), jnp.stack(v_rows),
            jnp.stack(s_prompt), jnp.stack(s_sample), jnp.stack(c_prompt), jnp.stack(c_sample))
```
